```python
import math
import jax, jax.numpy as jnp
from jax import lax
import numpy as np

D_MODEL = 2048
BATCH = 2
SEQ = 4096
DEPTH = 4

GLA_HEADS = 4
GLA_KEY_DIM = D_MODEL // 2
GLA_VAL_DIM = D_MODEL
GLA_HEAD_K = GLA_KEY_DIM // GLA_HEADS
GLA_HEAD_V = GLA_VAL_DIM // GLA_HEADS
GK_RANK = 16
GATE_LOGIT_NORMALIZER = 16.0
GLA_CHUNK = 64

DIFF_HEAD_DIM = 64
DIFF_HEADS = D_MODEL // (2 * DIFF_HEAD_DIM)
DIFF_QK_DIM = DIFF_HEADS * 2 * DIFF_HEAD_DIM
DIFF_VAL_DIM = DIFF_HEADS * 2 * DIFF_HEAD_DIM
Q_BLOCK = 128

SPLIT_SIZES = (
    GLA_KEY_DIM,
    GLA_KEY_DIM,
    GLA_VAL_DIM,
    GLA_VAL_DIM,
    GK_RANK,
    DIFF_QK_DIM,
    DIFF_QK_DIM,
    DIFF_VAL_DIM,
    DIFF_VAL_DIM,
    D_MODEL,
    D_MODEL,
)
N_IN = 4 * GLA_KEY_DIM // 2 + 2 * GLA_VAL_DIM + GK_RANK + 2 * DIFF_QK_DIM + 2 * DIFF_VAL_DIM + 2 * D_MODEL
EPS = 1e-6

kernel_name = "gla_diffattn_gated_hybrid"


def rmsnorm(x, g):
    xf = x.astype(jnp.float32)
    y = xf * lax.rsqrt(jnp.mean(xf * xf, axis=-1, keepdims=True) + EPS) * g.astype(jnp.float32)
    return y.astype(x.dtype)


def gla_chunked(q, k, v, gk):
    B, S, H, dk = q.shape
    dv = v.shape[-1]
    n = S // GLA_CHUNK

    def to_chunks(t):
        return t.reshape(B, n, GLA_CHUNK, H, t.shape[-1]).transpose(1, 0, 3, 2, 4)

    qc, kc, vc, gc = to_chunks(q * (dk ** -0.5)), to_chunks(k), to_chunks(v), to_chunks(gk)
    causal = jnp.tril(jnp.ones((GLA_CHUNK, GLA_CHUNK), dtype=bool))

    def step(state, inp):
        qi, ki, vi, gi = inp
        b = jnp.cumsum(gi, axis=-2)
        o_inter = jnp.einsum('bhtk,bhkv->bhtv', qi * jnp.exp(b), state)
        rel = b[:, :, :, None, :] - b[:, :, None, :, :]
        decay = jnp.exp(jnp.where(causal[:, :, None], rel, -jnp.inf))
        scores = jnp.einsum('bhtk,bhsk,bhtsk->bhts', qi, ki, decay)
        o = o_inter + jnp.einsum('bhts,bhsv->bhtv', scores, vi)
        b_last = b[:, :, -1:, :]
        state = jnp.exp(b_last[:, :, 0, :])[..., None] * state + jnp.einsum(
            'bhck,bhcv->bhkv', ki * jnp.exp(b_last - b), vi)
        return state, o

    state0 = jnp.zeros((B, H, dk, dv), jnp.float32)
    _, o = lax.scan(step, state0, (qc, kc, vc, gc))
    return o.transpose(1, 0, 3, 2, 4).reshape(B, S, H, dv)


def diff_attention(q, k, v, lam):
    B, S, H, _, d = q.shape
    nb = S // Q_BLOCK
    qb = (q * (d ** -0.5)).reshape(B, nb, Q_BLOCK, H, 2, d).transpose(1, 0, 2, 3, 4, 5)
    kpos = jnp.arange(S)

    def block(args):
        qi, i = args
        s = jnp.einsum('bqhmd,bkhmd->bhmqk', qi, k).astype(jnp.float32)
        qpos = i * Q_BLOCK + jnp.arange(Q_BLOCK)
        mask = kpos[None, :] <= qpos[:, None]
        p = jax.nn.softmax(jnp.where(mask, s, -jnp.inf), axis=-1)
        w = p[:, :, 0] - lam * p[:, :, 1]
        return jnp.einsum('bhqk,bkhv->bqhv', w.astype(v.dtype), v)

    o = lax.map(block, (qb, jnp.arange(nb)))
    return o.transpose(1, 0, 2, 3, 4).reshape(B, S, H, 2 * d)


def setup_inputs(seed: int = 0) -> dict:
    key = jax.random.key(seed)
    ks = jax.random.split(key, 13)
    f32 = jnp.float32
    x = jax.random.normal(ks[0], (BATCH, SEQ, D_MODEL), f32)
    pre_norm_g = 1.0 + 0.02 * jax.random.normal(ks[1], (DEPTH, D_MODEL), f32)
    post_norm_g = 1.0 + 0.02 * jax.random.normal(ks[2], (DEPTH, D_MODEL), f32)
    w_in = jax.random.normal(ks[3], (DEPTH, D_MODEL, N_IN), f32) * D_MODEL ** -0.5
    gla_gk_w2 = jax.random.normal(ks[4], (DEPTH, GK_RANK, GLA_KEY_DIM), f32) * GK_RANK ** -0.5
    gla_gk_b = 0.1 * jax.random.normal(ks[5], (DEPTH, GLA_KEY_DIM), f32)
    gla_norm_g = 1.0 + 0.02 * jax.random.normal(ks[6], (DEPTH, GLA_HEAD_V), f32)
    diff_lambda = 0.1 * jax.random.normal(ks[7], (DEPTH, 4, DIFF_HEAD_DIM), f32)
    diff_norm_g = 1.0 + 0.02 * jax.random.normal(ks[8], (DEPTH, 2 * DIFF_HEAD_DIM), f32)
    w_branch_a = jax.random.normal(ks[9], (DEPTH, GLA_VAL_DIM, D_MODEL), f32) * GLA_VAL_DIM ** -0.5
    w_branch_b = jax.random.normal(ks[10], (DEPTH, DIFF_VAL_DIM, D_MODEL), f32) * DIFF_VAL_DIM ** -0.5
    w_out = jax.random.normal(ks[11], (DEPTH, D_MODEL, D_MODEL), f32) * D_MODEL ** -0.5
    return {"x": x, "pre_norm_g": pre_norm_g, "post_norm_g": post_norm_g, "w_in": w_in,
            "gla_gk_w2": gla_gk_w2, "gla_gk_b": gla_gk_b, "gla_norm_g": gla_norm_g,
            "diff_lambda": diff_lambda, "diff_norm_g": diff_norm_g,
            "w_branch_a": w_branch_a, "w_branch_b": w_branch_b, "w_out": w_out}


def reference(x, pre_norm_g, post_norm_g, w_in, gla_gk_w2, gla_gk_b, gla_norm_g,
              diff_lambda, diff_norm_g, w_branch_a, w_branch_b, w_out):
    B, S, _ = x.shape
    offsets = list(np.cumsum(SPLIT_SIZES)[:-1])
    for l in range(DEPTH):
        h = rmsnorm(x, pre_norm_g[l])
        proj = h @ w_in[l]
        (a_q, a_k, a_v, a_g, a_lr, b_q, b_k, b_v, b_g, m_a, m_b) = jnp.split(proj, offsets, axis=-1)

        gk = jax.nn.log_sigmoid((a_lr @ gla_gk_w2[l] + gla_gk_b[l]).astype(jnp.float32)) / GATE_LOGIT_NORMALIZER
        o_a = gla_chunked(
            a_q.astype(jnp.float32).reshape(B, S, GLA_HEADS, GLA_HEAD_K),
            a_k.astype(jnp.float32).reshape(B, S, GLA_HEADS, GLA_HEAD_K),
            a_v.astype(jnp.float32).reshape(B, S, GLA_HEADS, GLA_HEAD_V),
            gk.reshape(B, S, GLA_HEADS, GLA_HEAD_K)).astype(h.dtype)
        o_a = rmsnorm(o_a, gla_norm_g[l]).reshape(B, S, GLA_VAL_DIM) * jax.nn.silu(a_g)
        y_a = o_a @ w_branch_a[l]

        lam_init = 0.8 - 0.6 * math.exp(-0.3 * l)
        lq1, lk1, lq2, lk2 = [diff_lambda[l, i].astype(jnp.float32) for i in range(4)]
        lam = jnp.exp(jnp.sum(lq1 * lk1)) - jnp.exp(jnp.sum(lq2 * lk2)) + lam_init
        o_b = diff_attention(
            b_q.reshape(B, S, DIFF_HEADS, 2, DIFF_HEAD_DIM),
            b_k.reshape(B, S, DIFF_HEADS, 2, DIFF_HEAD_DIM),
            b_v.reshape(B, S, DIFF_HEADS, 2 * DIFF_HEAD_DIM), lam)
        o_b = rmsnorm(o_b, diff_norm_g[l]) * (1.0 - lam_init)
        o_b = o_b.reshape(B, S, DIFF_VAL_DIM) * jax.nn.silu(b_g)
        y_b = o_b @ w_branch_b[l]

        merged = jax.nn.sigmoid(m_a) * y_a + jax.nn.sigmoid(m_b) * y_b
        out = merged @ w_out[l]
        x = x + rmsnorm(out, post_norm_g[l])
    return x
```

```python
import functools
import math

import jax
import jax.numpy as jnp
from jax import lax
from jax.experimental import pallas as pl
from jax.experimental.pallas import tpu as pltpu

F32 = jnp.float32
BF16 = jnp.bfloat16
EPS = 1e-6

GLA_HEAD_K = 256
GLA_HEAD_V = 512
GK_RANK = 16
GATE_LOGIT_NORMALIZER = 16.0
DIFF_HEAD_DIM = 64
DIFF_HEAD_V = 2 * DIFF_HEAD_DIM
LANES = 128

GLA_CHUNK = 64
ATTN_BLOCK = 256
VMEM_LIMIT_BYTES = 56 * 1024 * 1024

_NT = (((1,), (1,)), ((), ()))
_TN = (((0,), (0,)), ((), ()))


def _dot(a, b):
    return jnp.dot(a, b, preferred_element_type=F32)


def _sigmoid(x):
    return 1.0 / (1.0 + jnp.exp(-x))


def _params(*sem):
    return pltpu.CompilerParams(dimension_semantics=sem, vmem_limit_bytes=VMEM_LIMIT_BYTES)


def _inproj_kernel(x_ref, g_ref, w_ref, wlr_ref, o_ref, lr_ref, h_ref, *, row_chunk):
    @pl.when(pl.program_id(1) == 0)
    def _():
        def body(r, carry):
            sl = pl.ds(pl.multiple_of(r * row_chunk, row_chunk), row_chunk)
            xf = x_ref[sl, :]
            ms = jnp.mean(xf * xf, axis=-1, keepdims=True)
            h_ref[sl, :] = (xf * lax.rsqrt(ms + EPS) * g_ref[...]).astype(BF16)
            return carry
        lax.fori_loop(0, x_ref.shape[0] // row_chunk, body, 0)
        lr_ref[...] = _dot(h_ref[...], wlr_ref[...])

    o_ref[...] = _dot(h_ref[...], w_ref[...]).astype(o_ref.dtype)


def _inproj(x2, g, w_main, w_lr, *, tm, tn):
    m, d = x2.shape
    n = w_main.shape[1]
    return pl.pallas_call(
        functools.partial(_inproj_kernel, row_chunk=min(tm, 256)),
        grid=(m // tm, n // tn),
        in_specs=[
            pl.BlockSpec((tm, d), lambda i, j: (i, 0)),
            pl.BlockSpec((1, d), lambda i, j: (0, 0)),
            pl.BlockSpec((d, tn), lambda i, j: (0, j)),
            pl.BlockSpec((d, LANES), lambda i, j: (0, 0)),
        ],
        out_specs=[
            pl.BlockSpec((tm, tn), lambda i, j: (i, j)),
            pl.BlockSpec((tm, LANES), lambda i, j: (i, 0)),
        ],
        out_shape=[
            jax.ShapeDtypeStruct((m, n), BF16),
            jax.ShapeDtypeStruct((m, LANES), F32),
        ],
        scratch_shapes=[pltpu.VMEM((tm, d), BF16)],
        compiler_params=_params("parallel", "arbitrary"),
        name="inproj",
    )(x2, g, w_main, w_lr)


def _gla_kernel(q_ref, k_ref, v_ref, g_ref, lr_ref, w2_ref, b2_ref, ng_ref, o_ref, st_ref,
                *, heads, chunk):
    c = chunk

    @pl.when(pl.program_id(1) == 0)
    def _():
        st_ref[...] = jnp.zeros_like(st_ref)

    row = lax.broadcasted_iota(jnp.int32, (c, GLA_HEAD_K), 0)
    ti = lax.broadcasted_iota(jnp.int32, (c, c), 0)
    si = lax.broadcasted_iota(jnp.int32, (c, c), 1)
    tril = (ti >= si).astype(BF16)
    lr = lr_ref[...].astype(BF16)
    levels = int(math.log2(c))

    for hd in range(heads):
        ks = slice(hd * GLA_HEAD_K, (hd + 1) * GLA_HEAD_K)
        vs = slice(hd * GLA_HEAD_V, (hd + 1) * GLA_HEAD_V)
        z = _dot(lr, w2_ref[:, ks]) + b2_ref[:, ks]
        gk = (jnp.minimum(z, 0.0) - jnp.log1p(jnp.exp(-jnp.abs(z)))) * (1.0 / GATE_LOGIT_NORMALIZER)
        gk_hi = gk.astype(BF16)
        gk_lo = (gk - gk_hi.astype(F32)).astype(BF16)
        b = _dot(tril, gk_hi) + _dot(tril, gk_lo)

        qs = q_ref[:, ks].astype(F32) * (GLA_HEAD_K ** -0.5)
        kf = k_ref[:, ks].astype(F32)
        vb = v_ref[:, vs]

        a = jnp.where(ti == si, lax.dot_general(qs.astype(BF16), kf.astype(BF16), _NT,
                                                preferred_element_type=F32), 0.0)
        f = b
        for lv in range(levels):
            h = 1 << lv
            upper = (row & h) != 0
            e = jnp.where(upper, pltpu.roll(f, h, 0), f)
            d = jnp.where(upper, b - e, e - b)
            w = jnp.exp(jnp.minimum(d, 0.0))
            qe = jnp.where(upper, qs * w, 0.0).astype(BF16)
            ke = jnp.where(upper, 0.0, kf * w).astype(BF16)
            p = lax.dot_general(qe, ke, _NT, preferred_element_type=F32)
            a = a + jnp.where((ti >> (lv + 1)) == (si >> (lv + 1)), p, 0.0)
            if lv + 1 < levels:
                f = jnp.where(upper, f, pltpu.roll(f, c - h, 0))

        st = st_ref[hd]
        o = lax.dot_general((qs * jnp.exp(b)).astype(BF16), st.astype(BF16), _NT,
                            preferred_element_type=F32)
        o = o + _dot(a.astype(BF16), vb)

        b_last = b[c - 1:c, :]
        kd = (kf * jnp.exp(b_last - b)).astype(BF16)
        st_ref[hd] = st * jnp.exp(b_last) + lax.dot_general(vb, kd, _TN, preferred_element_type=F32)

        ms = jnp.mean(o * o, axis=-1, keepdims=True)
        gate = g_ref[:, vs].astype(F32)
        y = o * lax.rsqrt(ms + EPS) * ng_ref[...] * (gate * _sigmoid(gate))
        o_ref[:, vs] = y.astype(o_ref.dtype)


def _gla(proj, lr, w2, b2, ng, *, batch, seq, d_model, chunk):
    heads = (d_model // 2) // GLA_HEAD_K
    kd, vd = heads * GLA_HEAD_K, heads * GLA_HEAD_V
    nc = seq // chunk
    qb, kb, vb, gb = 0, 1, kd * 2 // vd, kd * 2 // vd + 1
    row = lambda bi, ci: bi * nc + ci
    return pl.pallas_call(
        functools.partial(_gla_kernel, heads=heads, chunk=chunk),
        grid=(batch, nc),
        in_specs=[
            pl.BlockSpec((chunk, kd), lambda bi, ci: (row(bi, ci), qb)),
            pl.BlockSpec((chunk, kd), lambda bi, ci: (row(bi, ci), kb)),
            pl.BlockSpec((chunk, vd), lambda bi, ci: (row(bi, ci), vb)),
            pl.BlockSpec((chunk, vd), lambda bi, ci: (row(bi, ci), gb)),
            pl.BlockSpec((chunk, LANES), lambda bi, ci: (row(bi, ci), 0)),
            pl.BlockSpec((LANES, kd), lambda bi, ci: (0, 0)),
            pl.BlockSpec((1, kd), lambda bi, ci: (0, 0)),
            pl.BlockSpec((1, GLA_HEAD_V), lambda bi, ci: (0, 0)),
        ],
        out_specs=pl.BlockSpec((chunk, vd), lambda bi, ci: (row(bi, ci), 0)),
        out_shape=jax.ShapeDtypeStruct((batch * seq, vd), BF16),
        scratch_shapes=[pltpu.VMEM((heads, GLA_HEAD_V, GLA_HEAD_K), F32)],
        compiler_params=_params("parallel", "arbitrary"),
        name="gla",
    )(proj, proj, proj, proj, lr, w2, b2, ng)


def _diff_attn_kernel(lam_ref, q_ref, k_ref, v_ref, g_ref, ng_ref, o_ref,
                      m1_ref, l1_ref, a1_ref, m2_ref, l2_ref, a2_ref, *, lam_init, blk):
    qi = pl.program_id(2)
    qf = q_ref[...].astype(F32) * (DIFF_HEAD_DIM ** -0.5)
    lane = lax.broadcasted_iota(jnp.int32, qf.shape, 1)
    q1 = jnp.where(lane < DIFF_HEAD_DIM, qf, 0.0).astype(BF16)
    q2 = jnp.where(lane >= DIFF_HEAD_DIM, qf, 0.0).astype(BF16)

    for m_ref, l_ref, a_ref in ((m1_ref, l1_ref, a1_ref), (m2_ref, l2_ref, a2_ref)):
        m_ref[...] = jnp.full_like(m_ref, -jnp.inf)
        l_ref[...] = jnp.zeros_like(l_ref)
        a_ref[...] = jnp.zeros_like(a_ref)

    def update(qm, kb, vb, m_ref, l_ref, a_ref, mask):
        s = lax.dot_general(qm, kb, _NT, preferred_element_type=F32)
        if mask is not None:
            s = jnp.where(mask, s, -jnp.inf)
        m_old = m_ref[...]
        m_new = jnp.maximum(m_old, jnp.max(s, axis=-1, keepdims=True))
        alpha = jnp.exp(m_old - m_new)
        p = jnp.exp(s - m_new)
        l_ref[...] = alpha * l_ref[...] + jnp.sum(p, axis=-1, keepdims=True)
        a_ref[...] = alpha * a_ref[...] + _dot(p.astype(BF16), vb)
        m_ref[...] = m_new

    def block(kb_idx, mask):
        sl = pl.ds(pl.multiple_of(kb_idx * blk, blk), blk)
        kb = k_ref[sl, :]
        vb = v_ref[sl, :]
        update(q1, kb, vb, m1_ref, l1_ref, a1_ref, mask)
        update(q2, kb, vb, m2_ref, l2_ref, a2_ref, mask)

    def body(kb_idx, carry):
        block(kb_idx, None)
        return carry

    lax.fori_loop(0, qi, body, 0)
    ti = lax.broadcasted_iota(jnp.int32, (blk, blk), 0)
    si = lax.broadcasted_iota(jnp.int32, (blk, blk), 1)
    block(qi, ti >= si)

    lam_p = lam_ref[...]
    lam = (jnp.exp(jnp.sum(lam_p[0:1] * lam_p[1:2], axis=-1, keepdims=True))
           - jnp.exp(jnp.sum(lam_p[2:3] * lam_p[3:4], axis=-1, keepdims=True)) + lam_init)
    o = a1_ref[...] / l1_ref[...] - lam * (a2_ref[...] / l2_ref[...])
    ms = jnp.mean(o * o, axis=-1, keepdims=True)
    gate = g_ref[...].astype(F32)
    y = o * lax.rsqrt(ms + EPS) * ng_ref[...] * (1.0 - lam_init) * (gate * _sigmoid(gate))
    o_ref[...] = y.astype(o_ref.dtype)


def _diff_attn(proj, lam_p, ng, *, batch, seq, d_model, lam_init, blk):
    heads = d_model // DIFF_HEAD_V
    nq = seq // blk
    per = d_model // DIFF_HEAD_V
    qb, kb, vb, gb = 3 * per, 4 * per, 5 * per, 6 * per
    stat = pltpu.VMEM((blk, 1), F32)
    acc = pltpu.VMEM((blk, DIFF_HEAD_V), F32)
    return pl.pallas_call(
        functools.partial(_diff_attn_kernel, lam_init=lam_init, blk=blk),
        grid=(batch, heads, nq),
        in_specs=[
            pl.BlockSpec((4, DIFF_HEAD_DIM), lambda b, h, i: (0, 0)),
            pl.BlockSpec((blk, DIFF_HEAD_V), lambda b, h, i: (b * nq + i, qb + h)),
            pl.BlockSpec((seq, DIFF_HEAD_V), lambda b, h, i: (b, kb + h)),
            pl.BlockSpec((seq, DIFF_HEAD_V), lambda b, h, i: (b, vb + h)),
            pl.BlockSpec((blk, DIFF_HEAD_V), lambda b, h, i: (b * nq + i, gb + h)),
            pl.BlockSpec((1, DIFF_HEAD_V), lambda b, h, i: (0, 0)),
        ],
        out_specs=pl.BlockSpec((blk, DIFF_HEAD_V), lambda b, h, i: (b * nq + i, h)),
        out_shape=jax.ShapeDtypeStruct((batch * seq, d_model), BF16),
        scratch_shapes=[stat, stat, acc, stat, stat, acc],
        compiler_params=_params("parallel", "parallel", "arbitrary"),
        name="diff_attn",
    )(lam_p, proj, proj, proj, proj, ng)


def _merge_kernel(oa_ref, ob_ref, wa_ref, wb_ref, ma_ref, mb_ref, o_ref):
    ya = _dot(oa_ref[...], wa_ref[...])
    yb = _dot(ob_ref[...], wb_ref[...])
    merged = _sigmoid(ma_ref[...].astype(F32)) * ya + _sigmoid(mb_ref[...].astype(F32)) * yb
    o_ref[...] = merged.astype(o_ref.dtype)


def _merge(o_a, o_b, w_a, w_b, proj, *, d_model, tm, tn):
    m = o_a.shape[0]
    nb = d_model // tn
    return pl.pallas_call(
        _merge_kernel,
        grid=(m // tm, nb),
        in_specs=[
            pl.BlockSpec((tm, d_model), lambda i, j: (i, 0)),
            pl.BlockSpec((tm, d_model), lambda i, j: (i, 0)),
            pl.BlockSpec((d_model, tn), lambda i, j: (0, j)),
            pl.BlockSpec((d_model, tn), lambda i, j: (0, j)),
            pl.BlockSpec((tm, tn), lambda i, j: (i, 7 * nb + j)),
            pl.BlockSpec((tm, tn), lambda i, j: (i, 8 * nb + j)),
        ],
        out_specs=pl.BlockSpec((tm, tn), lambda i, j: (i, j)),
        out_shape=jax.ShapeDtypeStruct((m, d_model), BF16),
        compiler_params=_params("parallel", "arbitrary"),
        name="merge",
    )(o_a, o_b, w_a, w_b, proj, proj)


def _outproj_kernel(m_ref, w_ref, g_ref, x_ref, o_ref):
    out = _dot(m_ref[...], w_ref[...])
    ms = jnp.mean(out * out, axis=-1, keepdims=True)
    o_ref[...] = x_ref[...] + out * lax.rsqrt(ms + EPS) * g_ref[...]


def _outproj(merged, w_out, g, x2, *, tm):
    m, d = x2.shape
    return pl.pallas_call(
        _outproj_kernel,
        grid=(m // tm,),
        in_specs=[
            pl.BlockSpec((tm, d), lambda i: (i, 0)),
            pl.BlockSpec((d, d), lambda i: (0, 0)),
            pl.BlockSpec((1, d), lambda i: (0, 0)),
            pl.BlockSpec((tm, d), lambda i: (i, 0)),
        ],
        out_specs=pl.BlockSpec((tm, d), lambda i: (i, 0)),
        out_shape=jax.ShapeDtypeStruct((m, d), F32),
        compiler_params=_params("parallel"),
        name="outproj",
    )(merged, w_out, g, x2)


def _tile(n, pref):
    t = min(n, pref)
    while n % t:
        t //= 2
    return t


def kernel(x, pre_norm_g, post_norm_g, w_in, gla_gk_w2, gla_gk_b, gla_norm_g, diff_lambda,
           diff_norm_g, w_branch_a, w_branch_b, w_out):
    batch, seq, d = x.shape
    depth = w_in.shape[0]
    m = batch * seq
    split = 3 * d

    w_main = jnp.concatenate([w_in[:, :, :split], w_in[:, :, split + GK_RANK:]], axis=-1).astype(BF16)
    w_lr = jnp.pad(w_in[:, :, split:split + GK_RANK], ((0, 0), (0, 0), (0, LANES - GK_RANK))).astype(BF16)
    w2 = jnp.pad(gla_gk_w2, ((0, 0), (0, LANES - GK_RANK), (0, 0))).astype(BF16)
    w_a = w_branch_a.astype(BF16)
    w_b = w_branch_b.astype(BF16)
    w_o = w_out.astype(BF16)

    tm = _tile(m, 1024)
    x2 = x.reshape(m, d)
    for l in range(depth):
        lam_init = 0.8 - 0.6 * math.exp(-0.3 * l)
        proj, lr = _inproj(x2, pre_norm_g[l][None], w_main[l], w_lr[l], tm=tm, tn=_tile(9 * d, 1024))
        o_a = _gla(proj, lr, w2[l], gla_gk_b[l][None], gla_norm_g[l][None],
                   batch=batch, seq=seq, d_model=d, chunk=_tile(seq, GLA_CHUNK))
        o_b = _diff_attn(proj, diff_lambda[l], diff_norm_g[l][None],
                         batch=batch, seq=seq, d_model=d, lam_init=lam_init, blk=_tile(seq, ATTN_BLOCK))
        merged = _merge(o_a, o_b, w_a[l], w_b[l], proj, d_model=d, tm=tm, tn=_tile(d, 512))
        x2 = _outproj(merged, w_o[l], post_norm_g[l][None], x2, tm=_tile(m, 512))
    return x2.reshape(batch, seq, d)
```

```python
import functools
import math

import jax
import jax.numpy as jnp
from jax import lax
from jax.experimental import pallas as pl
from jax.experimental.pallas import tpu as pltpu

F32 = jnp.float32
BF16 = jnp.bfloat16
EPS = 1e-6

GLA_HEAD_K = 256
GLA_HEAD_V = 512
GK_RANK = 16
GATE_LOGIT_NORMALIZER = 16.0
DIFF_HEAD_DIM = 64
DIFF_HEAD_V = 2 * DIFF_HEAD_DIM
LANES = 128

GLA_CHUNK = 64
ATTN_TQ = 512
ATTN_TK = 256
VMEM_LIMIT_BYTES = 56 * 1024 * 1024

_NT = (((1,), (1,)), ((), ()))
_TN = (((0,), (0,)), ((), ()))


def _dot(a, b):
    return jnp.dot(a, b, preferred_element_type=F32)


def _sigmoid(x):
    return 1.0 / (1.0 + jnp.exp(-x))


def _params(*sem):
    return pltpu.CompilerParams(dimension_semantics=sem, vmem_limit_bytes=VMEM_LIMIT_BYTES)


def _inproj_kernel(x_ref, g_ref, w_ref, wlr_ref, o_ref, lr_ref, h_ref, *, row_chunk):
    @pl.when(pl.program_id(1) == 0)
    def _():
        def body(r, carry):
            sl = pl.ds(pl.multiple_of(r * row_chunk, row_chunk), row_chunk)
            xf = x_ref[sl, :]
            ms = jnp.mean(xf * xf, axis=-1, keepdims=True)
            h_ref[sl, :] = (xf * lax.rsqrt(ms + EPS) * g_ref[...]).astype(BF16)
            return carry
        lax.fori_loop(0, x_ref.shape[0] // row_chunk, body, 0)
        lr_ref[...] = _dot(h_ref[...], wlr_ref[...])

    o_ref[...] = _dot(h_ref[...], w_ref[...]).astype(o_ref.dtype)


def _inproj(x2, g, w_main, w_lr, *, tm, tn):
    m, d = x2.shape
    n = w_main.shape[1]
    return pl.pallas_call(
        functools.partial(_inproj_kernel, row_chunk=min(tm, 256)),
        grid=(m // tm, n // tn),
        in_specs=[
            pl.BlockSpec((tm, d), lambda i, j: (i, 0)),
            pl.BlockSpec((1, d), lambda i, j: (0, 0)),
            pl.BlockSpec((d, tn), lambda i, j: (0, j)),
            pl.BlockSpec((d, LANES), lambda i, j: (0, 0)),
        ],
        out_specs=[
            pl.BlockSpec((tm, tn), lambda i, j: (i, j)),
            pl.BlockSpec((tm, LANES), lambda i, j: (i, 0)),
        ],
        out_shape=[
            jax.ShapeDtypeStruct((m, n), BF16),
            jax.ShapeDtypeStruct((m, LANES), F32),
        ],
        scratch_shapes=[pltpu.VMEM((tm, d), BF16)],
        compiler_params=_params("parallel", "arbitrary"),
        name="inproj",
    )(x2, g, w_main, w_lr)


def _gla_kernel(q_ref, k_ref, v_ref, g_ref, lr_ref, w2_ref, b2_ref, ng_ref, o_ref, st_ref,
                *, heads, chunk):
    c = chunk

    @pl.when(pl.program_id(1) == 0)
    def _():
        st_ref[...] = jnp.zeros_like(st_ref)

    row = lax.broadcasted_iota(jnp.int32, (c, GLA_HEAD_K), 0)
    ti = lax.broadcasted_iota(jnp.int32, (c, c), 0)
    si = lax.broadcasted_iota(jnp.int32, (c, c), 1)
    tril = (ti >= si).astype(BF16)
    lr = lr_ref[...].astype(BF16)
    levels = int(math.log2(c))

    for hd in range(heads):
        ks = slice(hd * GLA_HEAD_K, (hd + 1) * GLA_HEAD_K)
        vs = slice(hd * GLA_HEAD_V, (hd + 1) * GLA_HEAD_V)
        z = _dot(lr, w2_ref[:, ks]) + b2_ref[:, ks]
        gk = (jnp.minimum(z, 0.0) - jnp.log1p(jnp.exp(-jnp.abs(z)))) * (1.0 / GATE_LOGIT_NORMALIZER)
        gk_hi = gk.astype(BF16)
        gk_lo = (gk - gk_hi.astype(F32)).astype(BF16)
        b = _dot(tril, gk_hi) + _dot(tril, gk_lo)

        qs = q_ref[:, ks].astype(F32) * (GLA_HEAD_K ** -0.5)
        kf = k_ref[:, ks].astype(F32)
        vb = v_ref[:, vs]

        a = jnp.where(ti == si, lax.dot_general(qs.astype(BF16), kf.astype(BF16), _NT,
                                                preferred_element_type=F32), 0.0)
        f = b
        for lv in range(levels):
            h = 1 << lv
            upper = (row & h) != 0
            e = jnp.where(upper, pltpu.roll(f, h, 0), f)
            d = jnp.where(upper, b - e, e - b)
            w = jnp.exp(jnp.minimum(d, 0.0))
            qe = jnp.where(upper, qs * w, 0.0).astype(BF16)
            ke = jnp.where(upper, 0.0, kf * w).astype(BF16)
            p = lax.dot_general(qe, ke, _NT, preferred_element_type=F32)
            a = a + jnp.where((ti >> (lv + 1)) == (si >> (lv + 1)), p, 0.0)
            if lv + 1 < levels:
                f = jnp.where(upper, f, pltpu.roll(f, c - h, 0))

        st = st_ref[hd]
        o = lax.dot_general((qs * jnp.exp(b)).astype(BF16), st.astype(BF16), _NT,
                            preferred_element_type=F32)
        o = o + _dot(a.astype(BF16), vb)

        b_last = b[c - 1:c, :]
        kd = (kf * jnp.exp(b_last - b)).astype(BF16)
        st_ref[hd] = st * jnp.exp(b_last) + lax.dot_general(vb, kd, _TN, preferred_element_type=F32)

        ms = jnp.mean(o * o, axis=-1, keepdims=True)
        gate = g_ref[:, vs].astype(F32)
        y = o * lax.rsqrt(ms + EPS) * ng_ref[...] * (gate * _sigmoid(gate))
        o_ref[:, vs] = y.astype(o_ref.dtype)


def _gla(proj, lr, w2, b2, ng, *, batch, seq, d_model, chunk):
    heads = (d_model // 2) // GLA_HEAD_K
    kd, vd = heads * GLA_HEAD_K, heads * GLA_HEAD_V
    nc = seq // chunk
    qb, kb, vb, gb = 0, 1, kd * 2 // vd, kd * 2 // vd + 1
    row = lambda bi, ci: bi * nc + ci
    return pl.pallas_call(
        functools.partial(_gla_kernel, heads=heads, chunk=chunk),
        grid=(batch, nc),
        in_specs=[
            pl.BlockSpec((chunk, kd), lambda bi, ci: (row(bi, ci), qb)),
            pl.BlockSpec((chunk, kd), lambda bi, ci: (row(bi, ci), kb)),
            pl.BlockSpec((chunk, vd), lambda bi, ci: (row(bi, ci), vb)),
            pl.BlockSpec((chunk, vd), lambda bi, ci: (row(bi, ci), gb)),
            pl.BlockSpec((chunk, LANES), lambda bi, ci: (row(bi, ci), 0)),
            pl.BlockSpec((LANES, kd), lambda bi, ci: (0, 0)),
            pl.BlockSpec((1, kd), lambda bi, ci: (0, 0)),
            pl.BlockSpec((1, GLA_HEAD_V), lambda bi, ci: (0, 0)),
        ],
        out_specs=pl.BlockSpec((chunk, vd), lambda bi, ci: (row(bi, ci), 0)),
        out_shape=jax.ShapeDtypeStruct((batch * seq, vd), BF16),
        scratch_shapes=[pltpu.VMEM((heads, GLA_HEAD_V, GLA_HEAD_K), F32)],
        compiler_params=_params("parallel", "arbitrary"),
        name="gla",
    )(proj, proj, proj, proj, lr, w2, b2, ng)


def _diff_attn_kernel(lam_ref, q_ref, k_ref, v_ref, g_ref, ng_ref, o_ref,
                      m_ref, l_ref, a_ref, *, lam_init, tq, tk):
    seq = q_ref.shape[0]
    lam_p = lam_ref[...]
    lam = (jnp.exp(jnp.sum(lam_p[0:1] * lam_p[1:2], axis=-1, keepdims=True))
           - jnp.exp(jnp.sum(lam_p[2:3] * lam_p[3:4], axis=-1, keepdims=True)) + lam_init)
    lane = lax.broadcasted_iota(jnp.int32, (tq, DIFF_HEAD_V), 1)
    kpos = lax.broadcasted_iota(jnp.int32, (tk, tq), 0)
    qpos = lax.broadcasted_iota(jnp.int32, (tk, tq), 1)

    def q_block(i, carry):
        qsl = pl.ds(pl.multiple_of(i * tq, tq), tq)
        qf = q_ref[qsl, :].astype(F32) * (DIFF_HEAD_DIM ** -0.5)
        qm = (jnp.where(lane < DIFF_HEAD_DIM, qf, 0.0).astype(BF16),
              jnp.where(lane >= DIFF_HEAD_DIM, qf, 0.0).astype(BF16))
        m_ref[...] = jnp.full_like(m_ref, -jnp.inf)
        l_ref[...] = jnp.zeros_like(l_ref)
        a_ref[...] = jnp.zeros_like(a_ref)

        def kv_block(start, mask):
            ksl = pl.ds(pl.multiple_of(start, tk), tk)
            kb = k_ref[ksl, :]
            vb = v_ref[ksl, :]
            ss = [lax.dot_general(kb, qm[mp], _NT, preferred_element_type=F32) for mp in range(2)]
            for mp in range(2):
                s = ss[mp]
                if mask is not None:
                    s = jnp.where(mask, s, -jnp.inf)
                m_old = m_ref[mp]
                m_new = jnp.maximum(m_old, jnp.max(s, axis=0, keepdims=True))
                alpha = jnp.exp(m_old - m_new)
                p = jnp.exp(s - m_new)
                l_ref[mp] = alpha * l_ref[mp] + jnp.sum(p, axis=0, keepdims=True)
                a_ref[mp] = alpha * a_ref[mp] + lax.dot_general(vb, p.astype(BF16), _TN,
                                                                preferred_element_type=F32)
                m_ref[mp] = m_new

        def body(j, c):
            kv_block(j * tk, None)
            return c

        lax.fori_loop(0, i * (tq // tk), body, 0)
        for r in range(tq // tk):
            kv_block(i * tq + r * tk, (kpos + r * tk) <= qpos)

        o_t = a_ref[0] / l_ref[0] - lam * (a_ref[1] / l_ref[1])
        o = o_t.T
        ms = jnp.mean(o * o, axis=-1, keepdims=True)
        gate = g_ref[qsl, :].astype(F32)
        y = o * lax.rsqrt(ms + EPS) * ng_ref[...] * (1.0 - lam_init) * (gate * _sigmoid(gate))
        o_ref[qsl, :] = y.astype(o_ref.dtype)
        return carry

    lax.fori_loop(0, seq // tq, q_block, 0)


def _diff_attn(proj, lam_p, ng, *, batch, seq, d_model, lam_init, tq, tk):
    heads = d_model // DIFF_HEAD_V
    per = d_model // DIFF_HEAD_V
    qb, kb, vb, gb = 3 * per, 4 * per, 5 * per, 6 * per
    head_cols = lambda off: pl.BlockSpec((seq, DIFF_HEAD_V), lambda b, h: (b, off + h))
    return pl.pallas_call(
        functools.partial(_diff_attn_kernel, lam_init=lam_init, tq=tq, tk=tk),
        grid=(batch, heads),
        in_specs=[
            pl.BlockSpec((4, DIFF_HEAD_DIM), lambda b, h: (0, 0)),
            head_cols(qb), head_cols(kb), head_cols(vb), head_cols(gb),
            pl.BlockSpec((1, DIFF_HEAD_V), lambda b, h: (0, 0)),
        ],
        out_specs=head_cols(0),
        out_shape=jax.ShapeDtypeStruct((batch * seq, d_model), BF16),
        scratch_shapes=[pltpu.VMEM((2, 1, tq), F32), pltpu.VMEM((2, 1, tq), F32),
                        pltpu.VMEM((2, DIFF_HEAD_V, tq), F32)],
        compiler_params=_params("parallel", "parallel"),
        name="diff_attn",
    )(lam_p, proj, proj, proj, proj, ng)


def _merge_kernel(oa_ref, ob_ref, wa_ref, wb_ref, ma_ref, mb_ref, o_ref):
    ya = _dot(oa_ref[...], wa_ref[...])
    yb = _dot(ob_ref[...], wb_ref[...])
    merged = _sigmoid(ma_ref[...].astype(F32)) * ya + _sigmoid(mb_ref[...].astype(F32)) * yb
    o_ref[...] = merged.astype(o_ref.dtype)


def _merge(o_a, o_b, w_a, w_b, proj, *, d_model, tm, tn):
    m = o_a.shape[0]
    nb = d_model // tn
    return pl.pallas_call(
        _merge_kernel,
        grid=(m // tm, nb),
        in_specs=[
            pl.BlockSpec((tm, d_model), lambda i, j: (i, 0)),
            pl.BlockSpec((tm, d_model), lambda i, j: (i, 0)),
            pl.BlockSpec((d_model, tn), lambda i, j: (0, j)),
            pl.BlockSpec((d_model, tn), lambda i, j: (0, j)),
            pl.BlockSpec((tm, tn), lambda i, j: (i, 7 * nb + j)),
            pl.BlockSpec((tm, tn), lambda i, j: (i, 8 * nb + j)),
        ],
        out_specs=pl.BlockSpec((tm, tn), lambda i, j: (i, j)),
        out_shape=jax.ShapeDtypeStruct((m, d_model), BF16),
        compiler_params=_params("parallel", "arbitrary"),
        name="merge",
    )(o_a, o_b, w_a, w_b, proj, proj)


def _outproj_kernel(m_ref, w_ref, g_ref, x_ref, o_ref):
    out = _dot(m_ref[...], w_ref[...])
    ms = jnp.mean(out * out, axis=-1, keepdims=True)
    o_ref[...] = x_ref[...] + out * lax.rsqrt(ms + EPS) * g_ref[...]


def _outproj(merged, w_out, g, x2, *, tm):
    m, d = x2.shape
    return pl.pallas_call(
        _outproj_kernel,
        grid=(m // tm,),
        in_specs=[
            pl.BlockSpec((tm, d), lambda i: (i, 0)),
            pl.BlockSpec((d, d), lambda i: (0, 0)),
            pl.BlockSpec((1, d), lambda i: (0, 0)),
            pl.BlockSpec((tm, d), lambda i: (i, 0)),
        ],
        out_specs=pl.BlockSpec((tm, d), lambda i: (i, 0)),
        out_shape=jax.ShapeDtypeStruct((m, d), F32),
        compiler_params=_params("parallel"),
        name="outproj",
    )(merged, w_out, g, x2)


def _tile(n, pref):
    t = min(n, pref)
    while n % t:
        t //= 2
    return t


def kernel(x, pre_norm_g, post_norm_g, w_in, gla_gk_w2, gla_gk_b, gla_norm_g, diff_lambda,
           diff_norm_g, w_branch_a, w_branch_b, w_out):
    batch, seq, d = x.shape
    depth = w_in.shape[0]
    m = batch * seq
    split = 3 * d

    w_main = jnp.concatenate([w_in[:, :, :split], w_in[:, :, split + GK_RANK:]], axis=-1).astype(BF16)
    w_lr = jnp.pad(w_in[:, :, split:split + GK_RANK], ((0, 0), (0, 0), (0, LANES - GK_RANK))).astype(BF16)
    w2 = jnp.pad(gla_gk_w2, ((0, 0), (0, LANES - GK_RANK), (0, 0))).astype(BF16)
    w_a = w_branch_a.astype(BF16)
    w_b = w_branch_b.astype(BF16)
    w_o = w_out.astype(BF16)

    tm = _tile(m, 1024)
    x2 = x.reshape(m, d)
    for l in range(depth):
        lam_init = 0.8 - 0.6 * math.exp(-0.3 * l)
        proj, lr = _inproj(x2, pre_norm_g[l][None], w_main[l], w_lr[l], tm=tm, tn=_tile(9 * d, 1024))
        o_a = _gla(proj, lr, w2[l], gla_gk_b[l][None], gla_norm_g[l][None],
                   batch=batch, seq=seq, d_model=d, chunk=_tile(seq, GLA_CHUNK))
        o_b = _diff_attn(proj, diff_lambda[l], diff_norm_g[l][None],
                         batch=batch, seq=seq, d_model=d, lam_init=lam_init,
                         tq=_tile(seq, ATTN_TQ), tk=_tile(seq, ATTN_TK))
        merged = _merge(o_a, o_b, w_a[l], w_b[l], proj, d_model=d, tm=tm, tn=_tile(d, 512))
        x2 = _outproj(merged, w_o[l], post_norm_g[l][None], x2, tm=_tile(m, 512))
    return x2.reshape(batch, seq, d)
```

```python
import functools
import math

import jax
import jax.numpy as jnp
from jax import lax
from jax.experimental import pallas as pl
from jax.experimental.pallas import tpu as pltpu

F32 = jnp.float32
BF16 = jnp.bfloat16
EPS = 1e-6

GLA_HEAD_K = 256
GLA_HEAD_V = 512
GK_RANK = 16
GATE_LOGIT_NORMALIZER = 16.0
DIFF_HEAD_DIM = 64
DIFF_HEAD_V = 2 * DIFF_HEAD_DIM
LANES = 128
BF16_ROWS = 16

GLA_CHUNK = 64
ATTN_TQ = 512
ATTN_TK = 256
VMEM_LIMIT_BYTES = 56 * 1024 * 1024
LOG2E = math.log2(math.e)

_NT = (((1,), (1,)), ((), ()))
_TN = (((0,), (0,)), ((), ()))


def _dot(a, b):
    return jnp.dot(a, b, preferred_element_type=F32)


def _sigmoid(x):
    return 1.0 / (1.0 + jnp.exp(-x))


def _rmsnorm(x, g):
    return x * lax.rsqrt(jnp.mean(x * x, axis=-1, keepdims=True) + EPS) * g


def _params(*sem):
    return pltpu.CompilerParams(dimension_semantics=sem, vmem_limit_bytes=VMEM_LIMIT_BYTES)


def _cast_rows(dst_ref, load_rows, rows, chunk=256):
    chunk = min(chunk, rows)

    def body(r, carry):
        sl = pl.ds(pl.multiple_of(r * chunk, chunk), chunk)
        dst_ref[sl, :] = load_rows(sl).astype(BF16)
        return carry
    lax.fori_loop(0, rows // chunk, body, 0)


def _prenorm_kernel(x_ref, g_ref, h_ref):
    h_ref[...] = _rmsnorm(x_ref[...], g_ref[...]).astype(h_ref.dtype)


def _prenorm(x2, g, *, tm):
    m, d = x2.shape
    return pl.pallas_call(
        _prenorm_kernel,
        grid=(m // tm,),
        in_specs=[pl.BlockSpec((tm, d), lambda i: (i, 0)), pl.BlockSpec((1, d), lambda i: (0, 0))],
        out_specs=pl.BlockSpec((tm, d), lambda i: (i, 0)),
        out_shape=jax.ShapeDtypeStruct((m, d), BF16),
        compiler_params=_params("parallel"),
        name="prenorm",
    )(x2, g)


def _inproj_kernel(h_ref, wa_ref, wb_ref, o_ref, w_ref, *, n_aligned):
    j = pl.program_id(0)
    rows = wa_ref.shape[0]

    @pl.when(pl.program_id(1) == 0)
    def _():
        @pl.when(j < n_aligned)
        def _():
            _cast_rows(w_ref, lambda sl: wa_ref[sl, :], rows)

        @pl.when(j >= n_aligned)
        def _():
            _cast_rows(w_ref, lambda sl: jnp.concatenate(
                [wa_ref[sl, GK_RANK:], wb_ref[sl, :GK_RANK]], axis=1), rows)

    o_ref[...] = _dot(h_ref[...], w_ref[...]).astype(o_ref.dtype)


def _inproj(h, w_in_l, *, tm, tn):
    m, d = h.shape
    n_out = w_in_l.shape[1] - GK_RANK
    split = 3 * d
    return pl.pallas_call(
        functools.partial(_inproj_kernel, n_aligned=split // tn),
        grid=(n_out // tn, m // tm),
        in_specs=[
            pl.BlockSpec((tm, d), lambda j, i: (i, 0)),
            pl.BlockSpec((d, tn), lambda j, i: (0, j)),
            pl.BlockSpec((d, LANES), lambda j, i: (0, (j + 1) * (tn // LANES))),
        ],
        out_specs=pl.BlockSpec((tm, tn), lambda j, i: (i, j)),
        out_shape=jax.ShapeDtypeStruct((m, n_out), BF16),
        scratch_shapes=[pltpu.VMEM((d, tn), BF16)],
        compiler_params=_params("parallel", "arbitrary"),
        name="inproj",
    )(h, w_in_l, w_in_l)


def _lr_kernel(h_ref, w_ref, o_ref):
    o_ref[...] = _dot(h_ref[...], w_ref[...].astype(BF16))


def _lrproj(h, w_in_l, *, tm):
    m, d = h.shape
    return pl.pallas_call(
        _lr_kernel,
        grid=(m // tm,),
        in_specs=[pl.BlockSpec((tm, d), lambda i: (i, 0)),
                  pl.BlockSpec((d, LANES), lambda i: (0, 3 * d // LANES))],
        out_specs=pl.BlockSpec((tm, LANES), lambda i: (i, 0)),
        out_shape=jax.ShapeDtypeStruct((m, LANES), F32),
        compiler_params=_params("parallel"),
        name="lrproj",
    )(h, w_in_l)


def _gla_kernel(q_ref, k_ref, v_ref, g_ref, lr_ref, w2_ref, b2_ref, ng_ref, o_ref, st_ref,
                *, heads, chunk):
    c = chunk

    @pl.when(pl.program_id(1) == 0)
    def _():
        st_ref[...] = jnp.zeros_like(st_ref)

    row = lax.broadcasted_iota(jnp.int32, (c, GLA_HEAD_K), 0)
    ti = lax.broadcasted_iota(jnp.int32, (c, c), 0)
    si = lax.broadcasted_iota(jnp.int32, (c, c), 1)
    tril = (ti >= si).astype(BF16)
    lr = lr_ref[...].astype(BF16)
    levels = int(math.log2(c))

    for hd in range(heads):
        ks = slice(hd * GLA_HEAD_K, (hd + 1) * GLA_HEAD_K)
        vs = slice(hd * GLA_HEAD_V, (hd + 1) * GLA_HEAD_V)
        z = _dot(lr, w2_ref[:, ks]) + b2_ref[:, ks]
        gk = (jnp.minimum(z, 0.0) - jnp.log1p(jnp.exp(-jnp.abs(z)))) * (1.0 / GATE_LOGIT_NORMALIZER)
        gk_hi = gk.astype(BF16)
        gk_lo = (gk - gk_hi.astype(F32)).astype(BF16)
        b = _dot(tril, gk_hi) + _dot(tril, gk_lo)

        qs = q_ref[:, ks].astype(F32) * (GLA_HEAD_K ** -0.5)
        kf = k_ref[:, ks].astype(F32)
        vb = v_ref[:, vs]

        a = jnp.where(ti == si, lax.dot_general(qs.astype(BF16), kf.astype(BF16), _NT,
                                                preferred_element_type=F32), 0.0)
        f = b
        for lv in range(levels):
            h = 1 << lv
            upper = (row & h) != 0
            e = jnp.where(upper, pltpu.roll(f, h, 0), f)
            d = jnp.where(upper, b - e, e - b)
            w = jnp.exp(jnp.minimum(d, 0.0))
            qe = jnp.where(upper, qs * w, 0.0).astype(BF16)
            ke = jnp.where(upper, 0.0, kf * w).astype(BF16)
            p = lax.dot_general(qe, ke, _NT, preferred_element_type=F32)
            a = a + jnp.where((ti >> (lv + 1)) == (si >> (lv + 1)), p, 0.0)
            if lv + 1 < levels:
                f = jnp.where(upper, f, pltpu.roll(f, c - h, 0))

        st = st_ref[hd]
        o = lax.dot_general((qs * jnp.exp(b)).astype(BF16), st.astype(BF16), _NT,
                            preferred_element_type=F32)
        o = o + _dot(a.astype(BF16), vb)

        b_last = b[c - 1:c, :]
        kd = (kf * jnp.exp(b_last - b)).astype(BF16)
        st_ref[hd] = st * jnp.exp(b_last) + lax.dot_general(vb, kd, _TN, preferred_element_type=F32)

        gate = g_ref[:, vs].astype(F32)
        y = _rmsnorm(o, ng_ref[...]) * (gate * _sigmoid(gate))
        o_ref[:, vs] = y.astype(o_ref.dtype)


def _gla(proj, lr, w2, b2, ng, *, batch, seq, d_model, chunk):
    heads = (d_model // 2) // GLA_HEAD_K
    kd, vd = heads * GLA_HEAD_K, heads * GLA_HEAD_V
    nc = seq // chunk
    qb, kb, vb, gb = 0, 1, kd * 2 // vd, kd * 2 // vd + 1
    row = lambda bi, ci: bi * nc + ci
    return pl.pallas_call(
        functools.partial(_gla_kernel, heads=heads, chunk=chunk),
        grid=(batch, nc),
        in_specs=[
            pl.BlockSpec((chunk, kd), lambda bi, ci: (row(bi, ci), qb)),
            pl.BlockSpec((chunk, kd), lambda bi, ci: (row(bi, ci), kb)),
            pl.BlockSpec((chunk, vd), lambda bi, ci: (row(bi, ci), vb)),
            pl.BlockSpec((chunk, vd), lambda bi, ci: (row(bi, ci), gb)),
            pl.BlockSpec((chunk, LANES), lambda bi, ci: (row(bi, ci), 0)),
            pl.BlockSpec((LANES, kd), lambda bi, ci: (0, 0)),
            pl.BlockSpec((1, kd), lambda bi, ci: (0, 0)),
            pl.BlockSpec((1, GLA_HEAD_V), lambda bi, ci: (0, 0)),
        ],
        out_specs=pl.BlockSpec((chunk, vd), lambda bi, ci: (row(bi, ci), 0)),
        out_shape=jax.ShapeDtypeStruct((batch * seq, vd), BF16),
        scratch_shapes=[pltpu.VMEM((heads, GLA_HEAD_V, GLA_HEAD_K), F32)],
        compiler_params=_params("parallel", "arbitrary"),
        name="gla",
    )(proj, proj, proj, proj, lr, w2, b2, ng)


def _diff_attn_kernel(lam_ref, q_ref, k_ref, v_ref, g_ref, ng_ref, o_ref,
                      vt_ref, qm_ref, s_ref, p_ref, al_ref, m_ref, acc_ref, *, lam_init, tq, tk):
    seq = q_ref.shape[0]
    dv = DIFF_HEAD_V
    assert tq == 2 * tk

    vt_ref[:, dv:, :] = jnp.ones((seq // tk, BF16_ROWS, tk), BF16)

    def vt_body(c, carry):
        blk = v_ref[pl.ds(pl.multiple_of(c * tk, tk), tk), :]
        vt_ref[c, 0:dv, :] = blk.astype(F32).T.astype(BF16)
        return carry
    lax.fori_loop(0, seq // tk, vt_body, 0)

    lam_p = lam_ref[...]
    lam = (jnp.exp(jnp.sum(lam_p[0:1] * lam_p[1:2], axis=-1, keepdims=True))
           - jnp.exp(jnp.sum(lam_p[2:3] * lam_p[3:4], axis=-1, keepdims=True)) + lam_init)
    lane = lax.broadcasted_iota(jnp.int32, (tq, dv), 1)
    kpos = lax.broadcasted_iota(jnp.int32, (tk, tq), 0)
    qpos = lax.broadcasted_iota(jnp.int32, (tk, tq), 1)

    def qk(n, slot, mask):
        kb = k_ref[pl.ds(pl.multiple_of(n * tk, tk), tk), :]
        for mp in range(2):
            s = lax.dot_general(kb, qm_ref[mp], _NT, preferred_element_type=F32)
            if mask is not None:
                s = jnp.where(mask, s, -jnp.inf)
            s_ref[slot, mp] = s

    def softmax(slot):
        for mp in range(2):
            s = s_ref[slot, mp]
            m_old = m_ref[mp]
            m_new = jnp.maximum(m_old, jnp.max(s, axis=0, keepdims=True))
            al_ref[slot, mp] = jnp.exp2(m_old - m_new)
            p_ref[slot, mp] = jnp.exp2(s - m_new).astype(BF16)
            m_ref[mp] = m_new

    def pv(n, slot):
        vt = vt_ref[jnp.maximum(n, 0)]
        return [_dot(vt, p_ref[slot, mp]) for mp in range(2)]

    def accumulate(slot, pvs):
        for mp in range(2):
            acc_ref[mp] = al_ref[slot, mp] * acc_ref[mp] + pvs[mp]

    def step(n, parity, next_mask, has_next=True):
        pvs = pv(n - 1, 1 - parity)
        if has_next:
            qk(n + 1, 1 - parity, next_mask)
        softmax(parity)
        accumulate(1 - parity, pvs)

    def q_block(i, carry):
        qsl = pl.ds(pl.multiple_of(i * tq, tq), tq)
        qf = q_ref[qsl, :].astype(F32) * (DIFF_HEAD_DIM ** -0.5 * LOG2E)
        qm_ref[0] = jnp.where(lane < DIFF_HEAD_DIM, qf, 0.0).astype(BF16)
        qm_ref[1] = jnp.where(lane >= DIFF_HEAD_DIM, qf, 0.0).astype(BF16)
        m_ref[...] = jnp.full_like(m_ref, -jnp.inf)
        acc_ref[...] = jnp.zeros_like(acc_ref)
        al_ref[1] = jnp.ones_like(al_ref[1])
        p_ref[1] = jnp.zeros_like(p_ref[1])

        diag0 = kpos <= qpos
        diag1 = (kpos + tk) <= qpos
        qk(0, 0, jnp.logical_or(diag0, i > 0))

        def pair(pp, c):
            step(2 * pp, 0, None)
            step(2 * pp + 1, 1, None)
            return c
        lax.fori_loop(0, i - 1, pair, 0)

        @pl.when(i >= 1)
        def _():
            step(2 * i - 2, 0, None)
            step(2 * i - 1, 1, diag0)

        step(2 * i, 0, diag1)
        step(2 * i + 1, 1, None, has_next=False)
        accumulate(1, pv(2 * i + 1, 1))

        a0 = acc_ref[0]
        a1 = acc_ref[1]
        o_t = a0[0:dv] / a0[dv:dv + 1] - lam * (a1[0:dv] / a1[dv:dv + 1])
        gate = g_ref[qsl, :].astype(F32)
        y = _rmsnorm(o_t.T, ng_ref[...]) * (1.0 - lam_init) * (gate * _sigmoid(gate))
        o_ref[qsl, :] = y.astype(o_ref.dtype)
        return carry

    lax.fori_loop(0, seq // tq, q_block, 0)


def _diff_attn(proj, lam_p, ng, *, batch, seq, d_model, lam_init, tq, tk):
    heads = d_model // DIFF_HEAD_V
    per = d_model // DIFF_HEAD_V
    qb, kb, vb, gb = 3 * per, 4 * per, 5 * per, 6 * per
    head_cols = lambda off: pl.BlockSpec((seq, DIFF_HEAD_V), lambda b, h: (b, off + h))
    dve = DIFF_HEAD_V + BF16_ROWS
    return pl.pallas_call(
        functools.partial(_diff_attn_kernel, lam_init=lam_init, tq=tq, tk=tk),
        grid=(batch, heads),
        in_specs=[
            pl.BlockSpec((4, DIFF_HEAD_DIM), lambda b, h: (0, 0)),
            head_cols(qb), head_cols(kb), head_cols(vb), head_cols(gb),
            pl.BlockSpec((1, DIFF_HEAD_V), lambda b, h: (0, 0)),
        ],
        out_specs=head_cols(0),
        out_shape=jax.ShapeDtypeStruct((batch * seq, d_model), BF16),
        scratch_shapes=[
            pltpu.VMEM((seq // tk, dve, tk), BF16),
            pltpu.VMEM((2, tq, DIFF_HEAD_V), BF16),
            pltpu.VMEM((2, 2, tk, tq), F32),
            pltpu.VMEM((2, 2, tk, tq), BF16),
            pltpu.VMEM((2, 2, 1, tq), F32),
            pltpu.VMEM((2, 1, tq), F32),
            pltpu.VMEM((2, dve, tq), F32),
        ],
        compiler_params=_params("parallel", "parallel"),
        name="diff_attn",
    )(lam_p, proj, proj, proj, proj, ng)


def _merge_kernel(oa_ref, ob_ref, wa_ref, wb_ref, ma_ref, mb_ref, o_ref, w_ref):
    @pl.when(pl.program_id(1) == 0)
    def _():
        rows = wa_ref.shape[0]
        _cast_rows(w_ref.at[0], lambda sl: wa_ref[sl, :], rows)
        _cast_rows(w_ref.at[1], lambda sl: wb_ref[sl, :], rows)

    ya = _dot(oa_ref[...], w_ref[0])
    yb = _dot(ob_ref[...], w_ref[1])
    merged = _sigmoid(ma_ref[...].astype(F32)) * ya + _sigmoid(mb_ref[...].astype(F32)) * yb
    o_ref[...] = merged.astype(o_ref.dtype)


def _merge(o_a, o_b, w_a, w_b, proj, *, d_model, tm, tn):
    m = o_a.shape[0]
    nb = d_model // tn
    return pl.pallas_call(
        _merge_kernel,
        grid=(nb, m // tm),
        in_specs=[
            pl.BlockSpec((tm, d_model), lambda j, i: (i, 0)),
            pl.BlockSpec((tm, d_model), lambda j, i: (i, 0)),
            pl.BlockSpec((d_model, tn), lambda j, i: (0, j)),
            pl.BlockSpec((d_model, tn), lambda j, i: (0, j)),
            pl.BlockSpec((tm, tn), lambda j, i: (i, 7 * nb + j)),
            pl.BlockSpec((tm, tn), lambda j, i: (i, 8 * nb + j)),
        ],
        out_specs=pl.BlockSpec((tm, tn), lambda j, i: (i, j)),
        out_shape=jax.ShapeDtypeStruct((m, d_model), BF16),
        scratch_shapes=[pltpu.VMEM((2, d_model, tn), BF16)],
        compiler_params=_params("parallel", "arbitrary"),
        name="merge",
    )(o_a, o_b, w_a, w_b, proj, proj)


def _outproj_kernel(m_ref, w_ref, g_ref, gn_ref, x_ref, o_ref, *rest, emit_h):
    w_bf = rest[-1]

    @pl.when(pl.program_id(0) == 0)
    def _():
        _cast_rows(w_bf, lambda sl: w_ref[sl, :], w_ref.shape[0])

    x_new = x_ref[...] + _rmsnorm(_dot(m_ref[...], w_bf[...]), g_ref[...])
    o_ref[...] = x_new
    if emit_h:
        rest[0][...] = _rmsnorm(x_new, gn_ref[...]).astype(BF16)


def _outproj(merged, w_out, g, g_next, x2, *, tm, emit_h):
    m, d = x2.shape
    rows = pl.BlockSpec((tm, d), lambda i: (i, 0))
    vec = pl.BlockSpec((1, d), lambda i: (0, 0))
    out_specs = [rows] + ([rows] if emit_h else [])
    out_shape = [jax.ShapeDtypeStruct((m, d), F32)] + ([jax.ShapeDtypeStruct((m, d), BF16)] if emit_h else [])
    res = pl.pallas_call(
        functools.partial(_outproj_kernel, emit_h=emit_h),
        grid=(m // tm,),
        in_specs=[rows, pl.BlockSpec((d, d), lambda i: (0, 0), pipeline_mode=pl.Buffered(1)),
                  vec, vec, rows],
        out_specs=out_specs,
        out_shape=out_shape,
        scratch_shapes=[pltpu.VMEM((d, d), BF16)],
        compiler_params=_params("arbitrary"),
        name="outproj",
    )(merged, w_out, g, g_next, x2)
    return (res[0], res[1]) if emit_h else (res[0], None)


def _tile(n, pref):
    t = min(n, pref)
    while n % t:
        t //= 2
    return t


def kernel(x, pre_norm_g, post_norm_g, w_in, gla_gk_w2, gla_gk_b, gla_norm_g, diff_lambda,
           diff_norm_g, w_branch_a, w_branch_b, w_out):
    batch, seq, d = x.shape
    depth = w_in.shape[0]
    m = batch * seq
    w2 = jnp.pad(gla_gk_w2, ((0, 0), (0, LANES - GK_RANK), (0, 0))).astype(BF16)

    tm = _tile(m, 1024)
    x2 = x.reshape(m, d)
    h = _prenorm(x2, pre_norm_g[0][None], tm=_tile(m, 512))
    for l in range(depth):
        lam_init = 0.8 - 0.6 * math.exp(-0.3 * l)
        last = l + 1 == depth
        proj = _inproj(h, w_in[l], tm=tm, tn=_tile(3 * d, 1024))
        lr = _lrproj(h, w_in[l], tm=tm)
        o_a = _gla(proj, lr, w2[l], gla_gk_b[l][None], gla_norm_g[l][None],
                   batch=batch, seq=seq, d_model=d, chunk=_tile(seq, GLA_CHUNK))
        o_b = _diff_attn(proj, diff_lambda[l], diff_norm_g[l][None],
                         batch=batch, seq=seq, d_model=d, lam_init=lam_init,
                         tq=_tile(seq, ATTN_TQ), tk=_tile(seq, ATTN_TQ) // 2)
        merged = _merge(o_a, o_b, w_branch_a[l], w_branch_b[l], proj, d_model=d, tm=tm, tn=_tile(d, 512))
        x2, h = _outproj(merged, w_out[l], post_norm_g[l][None],
                         pre_norm_g[(l + 1) % depth][None], x2, tm=_tile(m, 256), emit_h=not last)
    return x2.reshape(batch, seq, d)
```

```python
import functools
import math

import jax
import jax.numpy as jnp
from jax import lax
from jax.experimental import pallas as pl
from jax.experimental.pallas import tpu as pltpu

F32 = jnp.float32
BF16 = jnp.bfloat16
EPS = 1e-6

GLA_HEAD_K = 256
GLA_HEAD_V = 512
GK_RANK = 16
GATE_LOGIT_NORMALIZER = 16.0
DIFF_HEAD_DIM = 64
DIFF_HEAD_V = 2 * DIFF_HEAD_DIM
LANES = 128
BF16_ROWS = 16

GLA_CHUNK = 128
ATTN_TQ = 512
VMEM_LIMIT_BYTES = 56 * 1024 * 1024
LOG2E = math.log2(math.e)

_NT = (((1,), (1,)), ((), ()))
_TN = (((0,), (0,)), ((), ()))


def _dot(a, b):
    return jnp.dot(a, b, preferred_element_type=F32)


def _sigmoid(x):
    return 1.0 / (1.0 + jnp.exp(-x))


def _rmsnorm(x, g):
    return x * lax.rsqrt(jnp.mean(x * x, axis=-1, keepdims=True) + EPS) * g


def _params(*sem):
    return pltpu.CompilerParams(dimension_semantics=sem, vmem_limit_bytes=VMEM_LIMIT_BYTES)


def _layer_row(l, n):
    return pl.BlockSpec((None, 1, n), lambda *_: (l, 0, 0))


def _cast_rows(dst_ref, load_rows, rows, chunk=256):
    chunk = min(chunk, rows)

    def body(r, carry):
        sl = pl.ds(pl.multiple_of(r * chunk, chunk), chunk)
        dst_ref[sl, :] = load_rows(sl).astype(BF16)
        return carry
    lax.fori_loop(0, rows // chunk, body, 0)


def _prenorm_kernel(x_ref, g_ref, h_ref):
    h_ref[...] = _rmsnorm(x_ref[...], g_ref[...]).astype(h_ref.dtype)


def _prenorm(x2, g, l, *, tm):
    m, d = x2.shape
    return pl.pallas_call(
        _prenorm_kernel,
        grid=(m // tm,),
        in_specs=[pl.BlockSpec((tm, d), lambda i: (i, 0)), _layer_row(l, d)],
        out_specs=pl.BlockSpec((tm, d), lambda i: (i, 0)),
        out_shape=jax.ShapeDtypeStruct((m, d), BF16),
        compiler_params=_params("parallel"),
        name="prenorm",
    )(x2, g)


def _inproj_kernel(h_ref, wa_ref, wb_ref, o_ref, w_ref, *, n_aligned):
    j = pl.program_id(0)
    rows = wa_ref.shape[0]

    @pl.when(pl.program_id(1) == 0)
    def _():
        @pl.when(j < n_aligned)
        def _():
            _cast_rows(w_ref, lambda sl: wa_ref[sl, :], rows)

        @pl.when(j >= n_aligned)
        def _():
            _cast_rows(w_ref, lambda sl: jnp.concatenate(
                [wa_ref[sl, GK_RANK:], wb_ref[sl, :GK_RANK]], axis=1), rows)

    o_ref[...] = _dot(h_ref[...], w_ref[...]).astype(o_ref.dtype)


def _inproj(h, w_in, l, *, tm, tn):
    m, d = h.shape
    n_out = w_in.shape[2] - GK_RANK
    split = 3 * d
    return pl.pallas_call(
        functools.partial(_inproj_kernel, n_aligned=split // tn),
        grid=(n_out // tn, m // tm),
        in_specs=[
            pl.BlockSpec((tm, d), lambda j, i: (i, 0)),
            pl.BlockSpec((None, d, tn), lambda j, i: (l, 0, j)),
            pl.BlockSpec((None, d, LANES), lambda j, i: (l, 0, (j + 1) * (tn // LANES))),
        ],
        out_specs=pl.BlockSpec((tm, tn), lambda j, i: (i, j)),
        out_shape=jax.ShapeDtypeStruct((m, n_out), BF16),
        scratch_shapes=[pltpu.VMEM((d, tn), BF16)],
        compiler_params=_params("parallel", "arbitrary"),
        name="inproj",
    )(h, w_in, w_in)


def _lr_kernel(h_ref, w_ref, o_ref):
    o_ref[...] = _dot(h_ref[...], w_ref[...].astype(BF16))


def _lrproj(h, w_in, l, *, tm):
    m, d = h.shape
    return pl.pallas_call(
        _lr_kernel,
        grid=(m // tm,),
        in_specs=[pl.BlockSpec((tm, d), lambda i: (i, 0)),
                  pl.BlockSpec((None, d, LANES), lambda i: (l, 0, 3 * d // LANES))],
        out_specs=pl.BlockSpec((tm, LANES), lambda i: (i, 0)),
        out_shape=jax.ShapeDtypeStruct((m, LANES), F32),
        compiler_params=_params("parallel"),
        name="lrproj",
    )(h, w_in)


def _gla_kernel(q_ref, k_ref, v_ref, g_ref, lr_ref, w2_ref, b2_ref, ng_ref, o_ref, st_ref,
                *, heads, chunk):
    c = chunk
    levels = int(math.log2(c))

    @pl.when(pl.program_id(1) == 0)
    def _():
        st_ref[...] = jnp.zeros_like(st_ref)

    head = lambda x, hd: x[:, hd * GLA_HEAD_K:(hd + 1) * GLA_HEAD_K]
    row = lax.broadcasted_iota(jnp.int32, (c, heads * GLA_HEAD_K), 0)
    ti = lax.broadcasted_iota(jnp.int32, (c, c), 0)
    si = lax.broadcasted_iota(jnp.int32, (c, c), 1)
    tril = (ti >= si).astype(BF16)

    z = _dot(lr_ref[...].astype(BF16), w2_ref[...]) + b2_ref[...]
    gk = (jnp.minimum(z, 0.0) - jnp.log(1.0 + jnp.exp(-jnp.abs(z)))) * (LOG2E / GATE_LOGIT_NORMALIZER)
    gk_hi = gk.astype(BF16)
    gk_lo = (gk - gk_hi.astype(F32)).astype(BF16)
    b = _dot(tril, gk_hi) + _dot(tril, gk_lo)

    kb = k_ref[...]
    qs = q_ref[...].astype(F32) * (GLA_HEAD_K ** -0.5)
    kf = kb.astype(F32)
    qb = qs.astype(BF16)

    a = [jnp.where(ti == si, lax.dot_general(head(qb, hd), head(kb, hd), _NT,
                                             preferred_element_type=F32), 0.0)
         for hd in range(heads)]
    f = b
    for lv in range(levels):
        h = 1 << lv
        upper = (row & h) != 0
        t = b - jnp.where(upper, pltpu.roll(f, h, 0), f)
        w = jnp.exp2(jnp.minimum(t, -t))
        qe = (qs * w).astype(BF16)
        ke = (kf * w).astype(BF16)
        pairs = jnp.logical_and(((ti ^ si) >> lv) == 1, ti > si)
        for hd in range(heads):
            p = lax.dot_general(head(qe, hd), head(ke, hd), _NT, preferred_element_type=F32)
            a[hd] = jnp.where(pairs, p, a[hd])
        if lv + 1 < levels:
            f = jnp.where(upper, f, pltpu.roll(f, c - h, 0))

    b_last = b[c - 1:c, :]
    qi = (qs * jnp.exp2(b)).astype(BF16)
    kd = (kf * jnp.exp2(b_last - b)).astype(BF16)
    decay = jnp.exp2(b_last)
    for hd in range(heads):
        vs = slice(hd * GLA_HEAD_V, (hd + 1) * GLA_HEAD_V)
        vb = v_ref[:, vs]
        st = st_ref[hd]
        o = lax.dot_general(head(qi, hd), st.astype(BF16), _NT, preferred_element_type=F32)
        o = o + _dot(a[hd].astype(BF16), vb)
        st_ref[hd] = st * head(decay, hd) + lax.dot_general(vb, head(kd, hd), _TN,
                                                            preferred_element_type=F32)
        gate = g_ref[:, vs].astype(F32)
        y = _rmsnorm(o, ng_ref[...]) * (gate * _sigmoid(gate))
        o_ref[:, vs] = y.astype(o_ref.dtype)


def _gla(proj, lr, w2, b2, ng, l, *, batch, seq, d_model, chunk):
    heads = (d_model // 2) // GLA_HEAD_K
    kd, vd = heads * GLA_HEAD_K, heads * GLA_HEAD_V
    nc = seq // chunk
    qb, kb, vb, gb = 0, 1, kd * 2 // vd, kd * 2 // vd + 1
    row = lambda bi, ci: bi * nc + ci
    return pl.pallas_call(
        functools.partial(_gla_kernel, heads=heads, chunk=chunk),
        grid=(batch, nc),
        in_specs=[
            pl.BlockSpec((chunk, kd), lambda bi, ci: (row(bi, ci), qb)),
            pl.BlockSpec((chunk, kd), lambda bi, ci: (row(bi, ci), kb)),
            pl.BlockSpec((chunk, vd), lambda bi, ci: (row(bi, ci), vb)),
            pl.BlockSpec((chunk, vd), lambda bi, ci: (row(bi, ci), gb)),
            pl.BlockSpec((chunk, LANES), lambda bi, ci: (row(bi, ci), 0)),
            pl.BlockSpec((None, LANES, kd), lambda bi, ci: (l, 0, 0)),
            _layer_row(l, kd),
            _layer_row(l, GLA_HEAD_V),
        ],
        out_specs=pl.BlockSpec((chunk, vd), lambda bi, ci: (row(bi, ci), 0)),
        out_shape=jax.ShapeDtypeStruct((batch * seq, vd), BF16),
        scratch_shapes=[pltpu.VMEM((heads, GLA_HEAD_V, GLA_HEAD_K), F32)],
        compiler_params=_params("parallel", "arbitrary"),
        name="gla",
    )(proj, proj, proj, proj, lr, w2, b2, ng)


def _diff_attn_kernel(lam_ref, q_ref, k_ref, v_ref, g_ref, ng_ref, o_ref,
                      vt_ref, qt_ref, s_ref, p_ref, al_ref, m_ref, acc_ref, *, lam_init, tq, tk):
    seq = q_ref.shape[0]
    dv = DIFF_HEAD_V
    nkb = seq // tk
    assert tq == 2 * tk

    vt_ref[0:nkb, dv:, :] = jnp.ones((nkb, BF16_ROWS, tk), BF16)
    vt_ref[nkb] = jnp.zeros(vt_ref.shape[1:], BF16)

    def vt_body(c, carry):
        blk = v_ref[pl.ds(pl.multiple_of(c * tk, tk), tk), :]
        vt_ref[c, 0:dv, :] = blk.astype(F32).T.astype(BF16)
        return carry
    lax.fori_loop(0, nkb, vt_body, 0)
    p_ref[1] = jnp.zeros_like(p_ref[1])
    acc_ref[...] = jnp.zeros_like(acc_ref)

    lam_p = lam_ref[...]
    lam = (jnp.exp(jnp.sum(lam_p[0:1] * lam_p[1:2], axis=-1, keepdims=True))
           - jnp.exp(jnp.sum(lam_p[2:3] * lam_p[3:4], axis=-1, keepdims=True)) + lam_init)
    drow = lax.broadcasted_iota(jnp.int32, (dv, tq), 0)
    kpos = lax.broadcasted_iota(jnp.int32, (tk, tq), 0)
    qpos = lax.broadcasted_iota(jnp.int32, (tk, tq), 1)

    def qk(n, slot, mask):
        kb = k_ref[pl.ds(pl.multiple_of(n * tk, tk), tk), :]
        for mp in range(2):
            s = _dot(kb, qt_ref[mp])
            if mask is not None:
                s = jnp.where(mask, s, -jnp.inf)
            s_ref[slot, mp] = s

    def softmax(slot):
        for mp in range(2):
            s = s_ref[slot, mp]
            m_old = m_ref[mp]
            m_new = jnp.maximum(m_old, jnp.max(s, axis=0, keepdims=True))
            al_ref[slot, mp] = jnp.exp2(m_old - m_new)
            p_ref[slot, mp] = jnp.exp2(s - m_new).astype(BF16)
            m_ref[mp] = m_new

    def pv(n, slot):
        vt = vt_ref[jnp.where(n < 0, nkb, n)]
        return [_dot(vt, p_ref[slot, mp]) for mp in range(2)]

    def accumulate(slot, pvs):
        for mp in range(2):
            acc_ref[mp] = al_ref[slot, mp] * acc_ref[mp] + pvs[mp]

    def step(n, parity, next_mask, has_next=True):
        pvs = pv(n - 1, 1 - parity)
        if has_next:
            qk(n + 1, 1 - parity, next_mask)
        softmax(parity)
        accumulate(1 - parity, pvs)

    def q_block(i, carry):
        qsl = pl.ds(pl.multiple_of(i * tq, tq), tq)
        qf_t = (q_ref[qsl, :].astype(F32) * (DIFF_HEAD_DIM ** -0.5 * LOG2E)).T
        qt_ref[0] = jnp.where(drow < DIFF_HEAD_DIM, qf_t, 0.0).astype(BF16)
        qt_ref[1] = jnp.where(drow >= DIFF_HEAD_DIM, qf_t, 0.0).astype(BF16)
        m_ref[...] = jnp.full_like(m_ref, -jnp.inf)
        al_ref[1] = jnp.zeros(al_ref.shape[1:], F32)

        diag0 = kpos <= qpos
        diag1 = (kpos + tk) <= qpos
        qk(0, 0, jnp.logical_or(diag0, i > 0))

        def pair(pp, c):
            step(2 * pp, 0, None)
            step(2 * pp + 1, 1, None)
            return c
        lax.fori_loop(0, i - 1, pair, 0)

        @pl.when(i >= 1)
        def _():
            step(2 * i - 2, 0, None)
            step(2 * i - 1, 1, diag0)

        step(2 * i, 0, diag1)
        step(2 * i + 1, 1, None, has_next=False)
        accumulate(1, pv(2 * i + 1, 1))

        a0 = acc_ref[0]
        a1 = acc_ref[1]
        o_t = a0[0:dv] / a0[dv:dv + 1] - lam * (a1[0:dv] / a1[dv:dv + 1])
        gate = g_ref[qsl, :].astype(F32)
        y = _rmsnorm(o_t.T, ng_ref[...]) * (1.0 - lam_init) * (gate * _sigmoid(gate))
        o_ref[qsl, :] = y.astype(o_ref.dtype)
        return carry

    lax.fori_loop(0, seq // tq, q_block, 0)


def _diff_attn(proj, lam_p, ng, l, *, batch, seq, d_model, lam_init, tq, tk):
    heads = d_model // DIFF_HEAD_V
    per = d_model // DIFF_HEAD_V
    qb, kb, vb, gb = 3 * per, 4 * per, 5 * per, 6 * per
    head_cols = lambda off: pl.BlockSpec((seq, DIFF_HEAD_V), lambda b, h: (b, off + h))
    dve = DIFF_HEAD_V + BF16_ROWS
    return pl.pallas_call(
        functools.partial(_diff_attn_kernel, lam_init=lam_init, tq=tq, tk=tk),
        grid=(batch, heads),
        in_specs=[
            pl.BlockSpec((None, 4, DIFF_HEAD_DIM), lambda b, h: (l, 0, 0)),
            head_cols(qb), head_cols(kb), head_cols(vb), head_cols(gb),
            _layer_row(l, DIFF_HEAD_V),
        ],
        out_specs=head_cols(0),
        out_shape=jax.ShapeDtypeStruct((batch * seq, d_model), BF16),
        scratch_shapes=[
            pltpu.VMEM((seq // tk + 1, dve, tk), BF16),
            pltpu.VMEM((2, DIFF_HEAD_V, tq), BF16),
            pltpu.VMEM((2, 2, tk, tq), F32),
            pltpu.VMEM((2, 2, tk, tq), BF16),
            pltpu.VMEM((2, 2, 1, tq), F32),
            pltpu.VMEM((2, 1, tq), F32),
            pltpu.VMEM((2, dve, tq), F32),
        ],
        compiler_params=_params("parallel", "parallel"),
        name="diff_attn",
    )(lam_p, proj, proj, proj, proj, ng)


def _merge_kernel(oa_ref, ob_ref, wa_ref, wb_ref, ma_ref, mb_ref, o_ref, w_ref):
    @pl.when(pl.program_id(1) == 0)
    def _():
        rows = wa_ref.shape[0]
        _cast_rows(w_ref.at[0], lambda sl: wa_ref[sl, :], rows)
        _cast_rows(w_ref.at[1], lambda sl: wb_ref[sl, :], rows)

    ya = _dot(oa_ref[...], w_ref[0])
    yb = _dot(ob_ref[...], w_ref[1])
    merged = _sigmoid(ma_ref[...].astype(F32)) * ya + _sigmoid(mb_ref[...].astype(F32)) * yb
    o_ref[...] = merged.astype(o_ref.dtype)


def _merge(o_a, o_b, w_a, w_b, proj, l, *, d_model, tm, tn):
    m = o_a.shape[0]
    nb = d_model // tn
    return pl.pallas_call(
        _merge_kernel,
        grid=(nb, m // tm),
        in_specs=[
            pl.BlockSpec((tm, d_model), lambda j, i: (i, 0)),
            pl.BlockSpec((tm, d_model), lambda j, i: (i, 0)),
            pl.BlockSpec((None, d_model, tn), lambda j, i: (l, 0, j)),
            pl.BlockSpec((None, d_model, tn), lambda j, i: (l, 0, j)),
            pl.BlockSpec((tm, tn), lambda j, i: (i, 7 * nb + j)),
            pl.BlockSpec((tm, tn), lambda j, i: (i, 8 * nb + j)),
        ],
        out_specs=pl.BlockSpec((tm, tn), lambda j, i: (i, j)),
        out_shape=jax.ShapeDtypeStruct((m, d_model), BF16),
        scratch_shapes=[pltpu.VMEM((2, d_model, tn), BF16)],
        compiler_params=_params("parallel", "arbitrary"),
        name="merge",
    )(o_a, o_b, w_a, w_b, proj, proj)


def _outproj_kernel(m_ref, w_ref, g_ref, gn_ref, x_ref, o_ref, *rest, emit_h):
    w_bf = rest[-1]

    @pl.when(pl.program_id(0) == 0)
    def _():
        _cast_rows(w_bf, lambda sl: w_ref[sl, :], w_ref.shape[0])

    x_new = x_ref[...] + _rmsnorm(_dot(m_ref[...], w_bf[...]), g_ref[...])
    o_ref[...] = x_new
    if emit_h:
        rest[0][...] = _rmsnorm(x_new, gn_ref[...]).astype(BF16)


def _outproj(merged, w_out, g, g_pre, x2, l, l_next, *, tm, emit_h):
    m, d = x2.shape
    rows = pl.BlockSpec((tm, d), lambda i: (i, 0))
    out_specs = [rows] + ([rows] if emit_h else [])
    out_shape = [jax.ShapeDtypeStruct((m, d), F32)] + ([jax.ShapeDtypeStruct((m, d), BF16)] if emit_h else [])
    res = pl.pallas_call(
        functools.partial(_outproj_kernel, emit_h=emit_h),
        grid=(m // tm,),
        in_specs=[rows,
                  pl.BlockSpec((None, d, d), lambda i: (l, 0, 0), pipeline_mode=pl.Buffered(1)),
                  _layer_row(l, d), _layer_row(l_next, d), rows],
        out_specs=out_specs,
        out_shape=out_shape,
        scratch_shapes=[pltpu.VMEM((d, d), BF16)],
        compiler_params=_params("arbitrary"),
        name="outproj",
    )(merged, w_out, g, g_pre, x2)
    return (res[0], res[1]) if emit_h else (res[0], None)


def _tile(n, pref):
    t = min(n, pref)
    while n % t:
        t //= 2
    return t


def kernel(x, pre_norm_g, post_norm_g, w_in, gla_gk_w2, gla_gk_b, gla_norm_g, diff_lambda,
           diff_norm_g, w_branch_a, w_branch_b, w_out):
    batch, seq, d = x.shape
    depth = w_in.shape[0]
    m = batch * seq
    w2 = jnp.pad(gla_gk_w2, ((0, 0), (0, LANES - GK_RANK), (0, 0))).astype(BF16)
    stack = lambda p: p.reshape(depth, 1, p.shape[-1])
    pre_g, post_g = stack(pre_norm_g), stack(post_norm_g)
    gk_b, gla_g, diff_g = stack(gla_gk_b), stack(gla_norm_g), stack(diff_norm_g)

    tm = _tile(m, 1024)
    tq = _tile(seq, ATTN_TQ)
    x2 = x.reshape(m, d)
    h = _prenorm(x2, pre_g, 0, tm=_tile(m, 512))
    for l in range(depth):
        lam_init = 0.8 - 0.6 * math.exp(-0.3 * l)
        last = l + 1 == depth
        proj = _inproj(h, w_in, l, tm=tm, tn=_tile(3 * d, 1024))
        lr = _lrproj(h, w_in, l, tm=tm)
        o_a = _gla(proj, lr, w2, gk_b, gla_g, l, batch=batch, seq=seq, d_model=d,
                   chunk=_tile(seq, GLA_CHUNK))
        o_b = _diff_attn(proj, diff_lambda, diff_g, l, batch=batch, seq=seq, d_model=d,
                         lam_init=lam_init, tq=tq, tk=tq // 2)
        merged = _merge(o_a, o_b, w_branch_a, w_branch_b, proj, l, d_model=d, tm=tm, tn=_tile(d, 512))
        x2, h = _outproj(merged, w_out, post_g, pre_g, x2, l, (l + 1) % depth,
                         tm=_tile(m, 256), emit_h=not last)
    return x2.reshape(batch, seq, d)
```

```python
import functools
import math

import jax
import jax.numpy as jnp
from jax import lax
from jax.experimental import pallas as pl
from jax.experimental.pallas import tpu as pltpu

F32 = jnp.float32
BF16 = jnp.bfloat16
EPS = 1e-6

GLA_HEAD_K = 256
GLA_HEAD_V = 512
GK_RANK = 16
GATE_LOGIT_NORMALIZER = 16.0
DIFF_HEAD_DIM = 64
DIFF_HEAD_V = 2 * DIFF_HEAD_DIM
LANES = 128
BF16_ROWS = 16

GLA_CHUNK = 128
ATTN_TQ = 512
VMEM_LIMIT_BYTES = 56 * 1024 * 1024
LOG2E = math.log2(math.e)

_NT = (((1,), (1,)), ((), ()))
_TN = (((0,), (0,)), ((), ()))


def _dot(a, b):
    return jnp.dot(a, b, preferred_element_type=F32)


def _sigmoid(x):
    return 1.0 / (1.0 + jnp.exp(-x))


def _rmsnorm(x, g):
    return x * lax.rsqrt(jnp.mean(x * x, axis=-1, keepdims=True) + EPS) * g


def _params(*sem):
    return pltpu.CompilerParams(dimension_semantics=sem, vmem_limit_bytes=VMEM_LIMIT_BYTES)


def _layer_row(l, n):
    return pl.BlockSpec((None, 1, n), lambda *_: (l, 0, 0))


def _cast_rows(dst_ref, load_rows, rows, chunk=256):
    chunk = min(chunk, rows)

    def body(r, carry):
        sl = pl.ds(pl.multiple_of(r * chunk, chunk), chunk)
        dst_ref[sl, :] = load_rows(sl).astype(BF16)
        return carry
    lax.fori_loop(0, rows // chunk, body, 0)


def _prenorm_kernel(x_ref, g_ref, h_ref):
    h_ref[...] = _rmsnorm(x_ref[...], g_ref[...]).astype(h_ref.dtype)


def _prenorm(x2, g, l, *, tm):
    m, d = x2.shape
    return pl.pallas_call(
        _prenorm_kernel,
        grid=(m // tm,),
        in_specs=[pl.BlockSpec((tm, d), lambda i: (i, 0)), _layer_row(l, d)],
        out_specs=pl.BlockSpec((tm, d), lambda i: (i, 0)),
        out_shape=jax.ShapeDtypeStruct((m, d), BF16),
        compiler_params=_params("parallel"),
        name="prenorm",
    )(x2, g)


def _inproj_kernel(h_ref, wa_ref, wb_ref, o_ref, w_ref, *, n_aligned):
    j = pl.program_id(0)
    tn = wa_ref.shape[0]
    chunk = min(256, tn)

    @pl.when(pl.program_id(1) == 0)
    def _():
        @pl.when(j < n_aligned)
        def _():
            for r in range(0, tn, chunk):
                w_ref[:, r:r + chunk] = wa_ref[r:r + chunk, :].T.astype(BF16)

        @pl.when(j >= n_aligned)
        def _():
            for r in range(0, tn - chunk, chunk):
                w_ref[:, r:r + chunk] = wa_ref[r + GK_RANK:r + chunk + GK_RANK, :].T.astype(BF16)
            tail = jnp.concatenate([wa_ref[tn - chunk + GK_RANK:tn, :], wb_ref[...]], axis=0)
            w_ref[:, tn - chunk:tn] = tail.T.astype(BF16)

    o_ref[...] = _dot(h_ref[...], w_ref[...]).astype(o_ref.dtype)


def _inproj(h, w_in_t, l, *, tm, tn):
    m, d = h.shape
    n_out = w_in_t.shape[1] - GK_RANK
    split = 3 * d
    return pl.pallas_call(
        functools.partial(_inproj_kernel, n_aligned=split // tn),
        grid=(n_out // tn, m // tm),
        in_specs=[
            pl.BlockSpec((tm, d), lambda j, i: (i, 0)),
            pl.BlockSpec((None, tn, d), lambda j, i: (l, j, 0)),
            pl.BlockSpec((None, GK_RANK, d), lambda j, i: (l, (j + 1) * (tn // GK_RANK), 0)),
        ],
        out_specs=pl.BlockSpec((tm, tn), lambda j, i: (i, j)),
        out_shape=jax.ShapeDtypeStruct((m, n_out), BF16),
        scratch_shapes=[pltpu.VMEM((d, tn), BF16)],
        compiler_params=_params("parallel", "arbitrary"),
        name="inproj",
    )(h, w_in_t, w_in_t)


def _lr_kernel(h_ref, w_ref, o_ref):
    o_ref[...] = lax.dot_general(h_ref[...], w_ref[...].astype(BF16), _NT,
                                 preferred_element_type=F32)


def _lrproj(h, w_in_t, l, *, tm):
    m, d = h.shape
    return pl.pallas_call(
        _lr_kernel,
        grid=(m // tm,),
        in_specs=[pl.BlockSpec((tm, d), lambda i: (i, 0)),
                  pl.BlockSpec((None, LANES, d), lambda i: (l, 3 * d // LANES, 0))],
        out_specs=pl.BlockSpec((tm, LANES), lambda i: (i, 0)),
        out_shape=jax.ShapeDtypeStruct((m, LANES), F32),
        compiler_params=_params("parallel"),
        name="lrproj",
    )(h, w_in_t)


def _gla_kernel(q_ref, k_ref, v_ref, g_ref, lr_ref, w2_ref, b2_ref, ng_ref, o_ref, st_ref,
                *, heads, chunk):
    c = chunk
    levels = int(math.log2(c))

    @pl.when(pl.program_id(1) == 0)
    def _():
        st_ref[...] = jnp.zeros_like(st_ref)

    head = lambda x, hd: x[:, hd * GLA_HEAD_K:(hd + 1) * GLA_HEAD_K]
    row = lax.broadcasted_iota(jnp.int32, (c, heads * GLA_HEAD_K), 0)
    ti = lax.broadcasted_iota(jnp.int32, (c, c), 0)
    si = lax.broadcasted_iota(jnp.int32, (c, c), 1)
    tril = (ti >= si).astype(BF16)

    z = _dot(lr_ref[...].astype(BF16), w2_ref[...]) + b2_ref[...]
    gk = (jnp.minimum(z, 0.0) - jnp.log(1.0 + jnp.exp(-jnp.abs(z)))) * (LOG2E / GATE_LOGIT_NORMALIZER)
    gk_hi = gk.astype(BF16)
    gk_lo = (gk - gk_hi.astype(F32)).astype(BF16)
    b = _dot(tril, gk_hi) + _dot(tril, gk_lo)

    kb = k_ref[...]
    qs = q_ref[...].astype(F32) * (GLA_HEAD_K ** -0.5)
    kf = kb.astype(F32)
    qb = qs.astype(BF16)

    a = [jnp.where(ti == si, lax.dot_general(head(qb, hd), head(kb, hd), _NT,
                                             preferred_element_type=F32), 0.0)
         for hd in range(heads)]
    f = b
    for lv in range(levels):
        h = 1 << lv
        upper = (row & h) != 0
        t = b - jnp.where(upper, pltpu.roll(f, h, 0), f)
        w = jnp.exp2(jnp.minimum(t, -t)).astype(BF16)
        qe = qb * w
        ke = kb * w
        pairs = jnp.logical_and(((ti ^ si) >> lv) == 1, ti > si)
        for hd in range(heads):
            p = lax.dot_general(head(qe, hd), head(ke, hd), _NT, preferred_element_type=F32)
            a[hd] = jnp.where(pairs, p, a[hd])
        if lv + 1 < levels:
            f = jnp.where(upper, f, pltpu.roll(f, c - h, 0))

    b_last = b[c - 1:c, :]
    qi = (qs * jnp.exp2(b)).astype(BF16)
    kd = (kf * jnp.exp2(b_last - b)).astype(BF16)
    decay = jnp.exp2(b_last)
    for hd in range(heads):
        vs = slice(hd * GLA_HEAD_V, (hd + 1) * GLA_HEAD_V)
        vb = v_ref[:, vs]
        st = st_ref[hd]
        o = lax.dot_general(head(qi, hd), st.astype(BF16), _NT, preferred_element_type=F32)
        o = o + _dot(a[hd].astype(BF16), vb)
        st_ref[hd] = st * head(decay, hd) + lax.dot_general(vb, head(kd, hd), _TN,
                                                            preferred_element_type=F32)
        gate = g_ref[:, vs].astype(F32)
        y = _rmsnorm(o, ng_ref[...]) * (gate * _sigmoid(gate))
        o_ref[:, vs] = y.astype(o_ref.dtype)


def _gla(proj, lr, w2, b2, ng, l, *, batch, seq, d_model, chunk):
    heads = (d_model // 2) // GLA_HEAD_K
    kd, vd = heads * GLA_HEAD_K, heads * GLA_HEAD_V
    nc = seq // chunk
    qb, kb, vb, gb = 0, 1, kd * 2 // vd, kd * 2 // vd + 1
    row = lambda bi, ci: bi * nc + ci
    return pl.pallas_call(
        functools.partial(_gla_kernel, heads=heads, chunk=chunk),
        grid=(batch, nc),
        in_specs=[
            pl.BlockSpec((chunk, kd), lambda bi, ci: (row(bi, ci), qb)),
            pl.BlockSpec((chunk, kd), lambda bi, ci: (row(bi, ci), kb)),
            pl.BlockSpec((chunk, vd), lambda bi, ci: (row(bi, ci), vb)),
            pl.BlockSpec((chunk, vd), lambda bi, ci: (row(bi, ci), gb)),
            pl.BlockSpec((chunk, LANES), lambda bi, ci: (row(bi, ci), 0)),
            pl.BlockSpec((None, LANES, kd), lambda bi, ci: (l, 0, 0)),
            _layer_row(l, kd),
            _layer_row(l, GLA_HEAD_V),
        ],
        out_specs=pl.BlockSpec((chunk, vd), lambda bi, ci: (row(bi, ci), 0)),
        out_shape=jax.ShapeDtypeStruct((batch * seq, vd), BF16),
        scratch_shapes=[pltpu.VMEM((heads, GLA_HEAD_V, GLA_HEAD_K), F32)],
        compiler_params=_params("parallel", "arbitrary"),
        name="gla",
    )(proj, proj, proj, proj, lr, w2, b2, ng)


def _diff_attn_kernel(lam_ref, q_ref, k_ref, v_ref, g_ref, ng_ref, o_ref,
                      vt_ref, qt_ref, s_ref, p_ref, al_ref, m_ref, acc_ref, *, lam_init, tq, tk):
    seq = q_ref.shape[0]
    dv = DIFF_HEAD_V
    nkb = seq // tk
    assert tq == 2 * tk

    vt_ref[0:nkb, dv:, :] = jnp.ones((nkb, BF16_ROWS, tk), BF16)
    vt_ref[nkb] = jnp.zeros(vt_ref.shape[1:], BF16)

    def vt_body(c, carry):
        blk = v_ref[pl.ds(pl.multiple_of(c * tk, tk), tk), :]
        vt_ref[c, 0:dv, :] = blk.astype(F32).T.astype(BF16)
        return carry
    lax.fori_loop(0, nkb, vt_body, 0)
    p_ref[1] = jnp.zeros_like(p_ref[1])
    acc_ref[...] = jnp.zeros_like(acc_ref)

    lam_p = lam_ref[...]
    lam = (jnp.exp(jnp.sum(lam_p[0:1] * lam_p[1:2], axis=-1, keepdims=True))
           - jnp.exp(jnp.sum(lam_p[2:3] * lam_p[3:4], axis=-1, keepdims=True)) + lam_init)
    drow = lax.broadcasted_iota(jnp.int32, (dv, tq), 0)
    kpos = lax.broadcasted_iota(jnp.int32, (tk, tq), 0)
    qpos = lax.broadcasted_iota(jnp.int32, (tk, tq), 1)

    def rows_at(start, size):
        return pl.ds(start if isinstance(start, int) else pl.multiple_of(start, size), size)

    def qk(n, slot, mask):
        kb = k_ref[rows_at(n * tk, tk), :]
        for mp in range(2):
            s = _dot(kb, qt_ref[mp])
            if mask is not None:
                s = jnp.where(mask, s, -jnp.inf)
            s_ref[slot, mp] = s

    def softmax(slot):
        for mp in range(2):
            s = s_ref[slot, mp]
            m_old = m_ref[mp]
            m_new = jnp.maximum(m_old, jnp.max(s, axis=0, keepdims=True))
            al_ref[slot, mp] = jnp.exp2(m_old - m_new)
            p_ref[slot, mp] = jnp.exp2(s - m_new).astype(BF16)
            m_ref[mp] = m_new

    def pv(n, slot):
        if isinstance(n, int):
            vt = vt_ref[nkb if n < 0 else n]
        else:
            vt = vt_ref[jnp.where(n < 0, nkb, n)]
        return [_dot(vt, p_ref[slot, mp]) for mp in range(2)]

    def accumulate(slot, pvs):
        for mp in range(2):
            acc_ref[mp] = al_ref[slot, mp] * acc_ref[mp] + pvs[mp]

    def step(n, parity, next_mask, has_next=True):
        pvs = pv(n - 1, 1 - parity)
        if has_next:
            qk(n + 1, 1 - parity, next_mask)
        softmax(parity)
        accumulate(1 - parity, pvs)

    def q_block(i, first):
        qsl = rows_at(i * tq, tq)
        qf_t = (q_ref[qsl, :].astype(F32) * (DIFF_HEAD_DIM ** -0.5 * LOG2E)).T
        qt_ref[0] = jnp.where(drow < DIFF_HEAD_DIM, qf_t, 0.0).astype(BF16)
        qt_ref[1] = jnp.where(drow >= DIFF_HEAD_DIM, qf_t, 0.0).astype(BF16)
        m_ref[...] = jnp.full_like(m_ref, -jnp.inf)
        al_ref[1] = jnp.zeros(al_ref.shape[1:], F32)

        diag0 = kpos <= qpos
        diag1 = (kpos + tk) <= qpos
        qk(0, 0, diag0 if first else None)

        if not first:
            def quad(qq, c):
                for u in range(4):
                    step(4 * qq + u, u % 2, None)
                return c
            quads = lax.shift_right_logical(i - 1, 1)
            lax.fori_loop(0, quads, quad, 0)

            @pl.when(((i - 1) & 1) == 1)
            def _():
                step(4 * quads, 0, None)
                step(4 * quads + 1, 1, None)

            step(2 * i - 2, 0, None)
            step(2 * i - 1, 1, diag0)

        step(2 * i, 0, diag1)
        step(2 * i + 1, 1, None, has_next=False)
        accumulate(1, pv(2 * i + 1, 1))

        a0 = acc_ref[0]
        a1 = acc_ref[1]
        o_t = a0[0:dv] / a0[dv:dv + 1] - lam * (a1[0:dv] / a1[dv:dv + 1])
        gate = g_ref[qsl, :].astype(F32)
        y = _rmsnorm(o_t.T, ng_ref[...]) * (1.0 - lam_init) * (gate * _sigmoid(gate))
        o_ref[qsl, :] = y.astype(o_ref.dtype)

    q_block(0, True)

    def later_block(i, carry):
        q_block(i, False)
        return carry
    lax.fori_loop(1, seq // tq, later_block, 0)


def _diff_attn(proj, lam_p, ng, l, *, batch, seq, d_model, lam_init, tq, tk):
    heads = d_model // DIFF_HEAD_V
    per = d_model // DIFF_HEAD_V
    qb, kb, vb, gb = 3 * per, 4 * per, 5 * per, 6 * per
    head_cols = lambda off: pl.BlockSpec((seq, DIFF_HEAD_V), lambda b, h: (b, off + h))
    dve = DIFF_HEAD_V + BF16_ROWS
    return pl.pallas_call(
        functools.partial(_diff_attn_kernel, lam_init=lam_init, tq=tq, tk=tk),
        grid=(batch, heads),
        in_specs=[
            pl.BlockSpec((None, 4, DIFF_HEAD_DIM), lambda b, h: (l, 0, 0)),
            head_cols(qb), head_cols(kb), head_cols(vb), head_cols(gb),
            _layer_row(l, DIFF_HEAD_V),
        ],
        out_specs=head_cols(0),
        out_shape=jax.ShapeDtypeStruct((batch * seq, d_model), BF16),
        scratch_shapes=[
            pltpu.VMEM((seq // tk + 1, dve, tk), BF16),
            pltpu.VMEM((2, DIFF_HEAD_V, tq), BF16),
            pltpu.VMEM((2, 2, tk, tq), F32),
            pltpu.VMEM((2, 2, tk, tq), BF16),
            pltpu.VMEM((2, 2, 1, tq), F32),
            pltpu.VMEM((2, 1, tq), F32),
            pltpu.VMEM((2, dve, tq), F32),
        ],
        compiler_params=_params("parallel", "parallel"),
        name="diff_attn",
    )(lam_p, proj, proj, proj, proj, ng)


def _merge_kernel(oa_ref, ob_ref, wa_ref, wb_ref, ma_ref, mb_ref, o_ref, w_ref):
    @pl.when(pl.program_id(1) == 0)
    def _():
        rows = wa_ref.shape[0]
        _cast_rows(w_ref.at[0], lambda sl: wa_ref[sl, :], rows)
        _cast_rows(w_ref.at[1], lambda sl: wb_ref[sl, :], rows)

    ya = _dot(oa_ref[...], w_ref[0])
    yb = _dot(ob_ref[...], w_ref[1])
    merged = _sigmoid(ma_ref[...].astype(F32)) * ya + _sigmoid(mb_ref[...].astype(F32)) * yb
    o_ref[...] = merged.astype(o_ref.dtype)


def _merge(o_a, o_b, w_a, w_b, proj, l, *, d_model, tm, tn):
    m = o_a.shape[0]
    nb = d_model // tn
    return pl.pallas_call(
        _merge_kernel,
        grid=(nb, m // tm),
        in_specs=[
            pl.BlockSpec((tm, d_model), lambda j, i: (i, 0)),
            pl.BlockSpec((tm, d_model), lambda j, i: (i, 0)),
            pl.BlockSpec((None, d_model, tn), lambda j, i: (l, 0, j)),
            pl.BlockSpec((None, d_model, tn), lambda j, i: (l, 0, j)),
            pl.BlockSpec((tm, tn), lambda j, i: (i, 7 * nb + j)),
            pl.BlockSpec((tm, tn), lambda j, i: (i, 8 * nb + j)),
        ],
        out_specs=pl.BlockSpec((tm, tn), lambda j, i: (i, j)),
        out_shape=jax.ShapeDtypeStruct((m, d_model), BF16),
        scratch_shapes=[pltpu.VMEM((2, d_model, tn), BF16)],
        compiler_params=_params("parallel", "arbitrary"),
        name="merge",
    )(o_a, o_b, w_a, w_b, proj, proj)


def _outproj_kernel(m_ref, w_ref, g_ref, gn_ref, x_ref, o_ref, *rest, emit_h):
    w_bf = rest[-1]

    @pl.when(pl.program_id(0) == 0)
    def _():
        _cast_rows(w_bf, lambda sl: w_ref[sl, :], w_ref.shape[0])

    x_new = x_ref[...] + _rmsnorm(_dot(m_ref[...], w_bf[...]), g_ref[...])
    o_ref[...] = x_new
    if emit_h:
        rest[0][...] = _rmsnorm(x_new, gn_ref[...]).astype(BF16)


def _outproj(merged, w_out, g, g_pre, x2, l, l_next, *, tm, emit_h):
    m, d = x2.shape
    rows = pl.BlockSpec((tm, d), lambda i: (i, 0))
    out_specs = [rows] + ([rows] if emit_h else [])
    out_shape = [jax.ShapeDtypeStruct((m, d), F32)] + ([jax.ShapeDtypeStruct((m, d), BF16)] if emit_h else [])
    res = pl.pallas_call(
        functools.partial(_outproj_kernel, emit_h=emit_h),
        grid=(m // tm,),
        in_specs=[rows,
                  pl.BlockSpec((None, d, d), lambda i: (l, 0, 0), pipeline_mode=pl.Buffered(1)),
                  _layer_row(l, d), _layer_row(l_next, d), rows],
        out_specs=out_specs,
        out_shape=out_shape,
        scratch_shapes=[pltpu.VMEM((d, d), BF16)],
        compiler_params=_params("arbitrary"),
        name="outproj",
    )(merged, w_out, g, g_pre, x2)
    return (res[0], res[1]) if emit_h else (res[0], None)


def _tile(n, pref):
    t = min(n, pref)
    while n % t:
        t //= 2
    return t


def kernel(x, pre_norm_g, post_norm_g, w_in, gla_gk_w2, gla_gk_b, gla_norm_g, diff_lambda,
           diff_norm_g, w_branch_a, w_branch_b, w_out):
    batch, seq, d = x.shape
    depth = w_in.shape[0]
    m = batch * seq
    w2 = jnp.pad(gla_gk_w2, ((0, 0), (0, LANES - GK_RANK), (0, 0))).astype(BF16)
    w_in_t = jnp.swapaxes(w_in, 1, 2)
    stack = lambda p: p.reshape(depth, 1, p.shape[-1])
    pre_g, post_g = stack(pre_norm_g), stack(post_norm_g)
    gk_b, gla_g, diff_g = stack(gla_gk_b), stack(gla_norm_g), stack(diff_norm_g)

    tm = _tile(m, 1024)
    tq = _tile(seq, ATTN_TQ)
    x2 = x.reshape(m, d)
    h = _prenorm(x2, pre_g, 0, tm=_tile(m, 512))
    for l in range(depth):
        lam_init = 0.8 - 0.6 * math.exp(-0.3 * l)
        last = l + 1 == depth
        proj = _inproj(h, w_in_t, l, tm=tm, tn=_tile(3 * d, 1024))
        lr = _lrproj(h, w_in_t, l, tm=tm)
        o_a = _gla(proj, lr, w2, gk_b, gla_g, l, batch=batch, seq=seq, d_model=d,
                   chunk=_tile(seq, GLA_CHUNK))
        o_b = _diff_attn(proj, diff_lambda, diff_g, l, batch=batch, seq=seq, d_model=d,
                         lam_init=lam_init, tq=tq, tk=tq // 2)
        merged = _merge(o_a, o_b, w_branch_a, w_branch_b, proj, l, d_model=d, tm=tm, tn=_tile(d, 512))
        x2, h = _outproj(merged, w_out, post_g, pre_g, x2, l, (l + 1) % depth,
                         tm=_tile(m, 256), emit_h=not last)
    return x2.reshape(batch, seq, d)
```

```python
import functools
import math

import jax
import jax.numpy as jnp
from jax import lax
from jax.experimental import pallas as pl
from jax.experimental.pallas import tpu as pltpu

F32 = jnp.float32
BF16 = jnp.bfloat16
EPS = 1e-6

GLA_HEAD_K = 256
GLA_HEAD_V = 512
GK_RANK = 16
GATE_LOGIT_NORMALIZER = 16.0
DIFF_HEAD_DIM = 64
DIFF_HEAD_V = 2 * DIFF_HEAD_DIM
LANES = 128
BF16_ROWS = 16

GLA_CHUNK = 128
ATTN_TQ = 512
VMEM_LIMIT_BYTES = 56 * 1024 * 1024
LOG2E = math.log2(math.e)

_NT = (((1,), (1,)), ((), ()))
_TN = (((0,), (0,)), ((), ()))


def _dot(a, b):
    return jnp.dot(a, b, preferred_element_type=F32)


def _sigmoid(x):
    return 1.0 / (1.0 + jnp.exp(-x))


def _rmsnorm(x, g):
    return x * lax.rsqrt(jnp.mean(x * x, axis=-1, keepdims=True) + EPS) * g


def _params(*sem):
    return pltpu.CompilerParams(dimension_semantics=sem, vmem_limit_bytes=VMEM_LIMIT_BYTES)


def _layer_row(l, n):
    return pl.BlockSpec((None, 1, n), lambda *_: (l, 0, 0))


def _cast_rows(dst_ref, load_rows, rows, chunk=256):
    chunk = min(chunk, rows)

    def body(r, carry):
        sl = pl.ds(pl.multiple_of(r * chunk, chunk), chunk)
        dst_ref[sl, :] = load_rows(sl).astype(BF16)
        return carry
    lax.fori_loop(0, rows // chunk, body, 0)


def _prenorm_kernel(x_ref, g_ref, h_ref):
    h_ref[...] = _rmsnorm(x_ref[...], g_ref[...]).astype(h_ref.dtype)


def _prenorm(x2, g, l, *, tm):
    m, d = x2.shape
    return pl.pallas_call(
        _prenorm_kernel,
        grid=(m // tm,),
        in_specs=[pl.BlockSpec((tm, d), lambda i: (i, 0)), _layer_row(l, d)],
        out_specs=pl.BlockSpec((tm, d), lambda i: (i, 0)),
        out_shape=jax.ShapeDtypeStruct((m, d), BF16),
        compiler_params=_params("parallel"),
        name="prenorm",
    )(x2, g)


def _inproj_kernel(h_ref, wa_ref, wb_ref, o_ref, w_ref, *, n_aligned):
    j = pl.program_id(0)
    tn = wa_ref.shape[0]
    chunk = min(256, tn)

    @pl.when(pl.program_id(1) == 0)
    def _():
        @pl.when(j < n_aligned)
        def _():
            for r in range(0, tn, chunk):
                w_ref[:, r:r + chunk] = wa_ref[r:r + chunk, :].T.astype(BF16)

        @pl.when(j >= n_aligned)
        def _():
            for r in range(0, tn - chunk, chunk):
                w_ref[:, r:r + chunk] = wa_ref[r + GK_RANK:r + chunk + GK_RANK, :].T.astype(BF16)
            tail = jnp.concatenate([wa_ref[tn - chunk + GK_RANK:tn, :], wb_ref[...]], axis=0)
            w_ref[:, tn - chunk:tn] = tail.T.astype(BF16)

    o_ref[...] = _dot(h_ref[...], w_ref[...]).astype(o_ref.dtype)


def _inproj(h, w_in_t, l, *, tm, tn):
    m, d = h.shape
    n_out = w_in_t.shape[1] - GK_RANK
    split = 3 * d
    return pl.pallas_call(
        functools.partial(_inproj_kernel, n_aligned=split // tn),
        grid=(n_out // tn, m // tm),
        in_specs=[
            pl.BlockSpec((tm, d), lambda j, i: (i, 0)),
            pl.BlockSpec((None, tn, d), lambda j, i: (l, j, 0)),
            pl.BlockSpec((None, GK_RANK, d), lambda j, i: (l, (j + 1) * (tn // GK_RANK), 0)),
        ],
        out_specs=pl.BlockSpec((tm, tn), lambda j, i: (i, j)),
        out_shape=jax.ShapeDtypeStruct((m, n_out), BF16),
        scratch_shapes=[pltpu.VMEM((d, tn), BF16)],
        compiler_params=_params("parallel", "arbitrary"),
        name="inproj",
    )(h, w_in_t, w_in_t)


def _lr_kernel(h_ref, w_ref, o_ref):
    o_ref[...] = lax.dot_general(h_ref[...], w_ref[...].astype(BF16), _NT,
                                 preferred_element_type=F32)


def _lrproj(h, w_in_t, l, *, tm):
    m, d = h.shape
    return pl.pallas_call(
        _lr_kernel,
        grid=(m // tm,),
        in_specs=[pl.BlockSpec((tm, d), lambda i: (i, 0)),
                  pl.BlockSpec((None, LANES, d), lambda i: (l, 3 * d // LANES, 0))],
        out_specs=pl.BlockSpec((tm, LANES), lambda i: (i, 0)),
        out_shape=jax.ShapeDtypeStruct((m, LANES), F32),
        compiler_params=_params("parallel"),
        name="lrproj",
    )(h, w_in_t)


def _gla_kernel(q_ref, k_ref, v_ref, g_ref, lr_ref, w2_ref, b2_ref, ng_ref, o_ref, st_ref,
                *, heads, chunk):
    c = chunk
    levels = int(math.log2(c))

    @pl.when(pl.program_id(1) == 0)
    def _():
        st_ref[...] = jnp.zeros_like(st_ref)

    head = lambda x, hd: x[:, hd * GLA_HEAD_K:(hd + 1) * GLA_HEAD_K]
    row = lax.broadcasted_iota(jnp.int32, (c, heads * GLA_HEAD_K), 0)
    ti = lax.broadcasted_iota(jnp.int32, (c, c), 0)
    si = lax.broadcasted_iota(jnp.int32, (c, c), 1)
    tril = (ti >= si).astype(BF16)

    z = _dot(lr_ref[...].astype(BF16), w2_ref[...]) + b2_ref[...]
    gk = (jnp.minimum(z, 0.0) - jnp.log(1.0 + jnp.exp(-jnp.abs(z)))) * (LOG2E / GATE_LOGIT_NORMALIZER)
    gk_hi = gk.astype(BF16)
    gk_lo = (gk - gk_hi.astype(F32)).astype(BF16)
    b = _dot(tril, gk_hi) + _dot(tril, gk_lo)

    kb = k_ref[...]
    qs = q_ref[...].astype(F32) * (GLA_HEAD_K ** -0.5)
    kf = kb.astype(F32)
    qb = qs.astype(BF16)

    a = [jnp.where(ti == si, lax.dot_general(head(qb, hd), head(kb, hd), _NT,
                                             preferred_element_type=F32), 0.0)
         for hd in range(heads)]
    f = b
    for lv in range(levels):
        h = 1 << lv
        upper = (row & h) != 0
        t = b - jnp.where(upper, pltpu.roll(f, h, 0), f)
        w = jnp.exp2(jnp.minimum(t, -t)).astype(BF16)
        qe = qb * w
        ke = kb * w
        pairs = jnp.logical_and(((ti ^ si) >> lv) == 1, ti > si)
        for hd in range(heads):
            p = lax.dot_general(head(qe, hd), head(ke, hd), _NT, preferred_element_type=F32)
            a[hd] = jnp.where(pairs, p, a[hd])
        if lv + 1 < levels:
            f = jnp.where(upper, f, pltpu.roll(f, c - h, 0))

    b_last = b[c - 1:c, :]
    qi = (qs * jnp.exp2(b)).astype(BF16)
    kd = (kf * jnp.exp2(b_last - b)).astype(BF16)
    decay = jnp.exp2(b_last)
    for hd in range(heads):
        vs = slice(hd * GLA_HEAD_V, (hd + 1) * GLA_HEAD_V)
        vb = v_ref[:, vs]
        st = st_ref[hd]
        o = lax.dot_general(head(qi, hd), st.astype(BF16), _NT, preferred_element_type=F32)
        o = o + _dot(a[hd].astype(BF16), vb)
        st_ref[hd] = st * head(decay, hd) + lax.dot_general(vb, head(kd, hd), _TN,
                                                            preferred_element_type=F32)
        gate = g_ref[:, vs].astype(F32)
        y = _rmsnorm(o, ng_ref[...]) * (gate * _sigmoid(gate))
        o_ref[:, vs] = y.astype(o_ref.dtype)


def _gla(proj, lr, w2, b2, ng, l, *, batch, seq, d_model, chunk):
    heads = (d_model // 2) // GLA_HEAD_K
    kd, vd = heads * GLA_HEAD_K, heads * GLA_HEAD_V
    nc = seq // chunk
    qb, kb, vb, gb = 0, 1, kd * 2 // vd, kd * 2 // vd + 1
    row = lambda bi, ci: bi * nc + ci
    return pl.pallas_call(
        functools.partial(_gla_kernel, heads=heads, chunk=chunk),
        grid=(batch, nc),
        in_specs=[
            pl.BlockSpec((chunk, kd), lambda bi, ci: (row(bi, ci), qb)),
            pl.BlockSpec((chunk, kd), lambda bi, ci: (row(bi, ci), kb)),
            pl.BlockSpec((chunk, vd), lambda bi, ci: (row(bi, ci), vb)),
            pl.BlockSpec((chunk, vd), lambda bi, ci: (row(bi, ci), gb)),
            pl.BlockSpec((chunk, LANES), lambda bi, ci: (row(bi, ci), 0)),
            pl.BlockSpec((None, LANES, kd), lambda bi, ci: (l, 0, 0)),
            _layer_row(l, kd),
            _layer_row(l, GLA_HEAD_V),
        ],
        out_specs=pl.BlockSpec((chunk, vd), lambda bi, ci: (row(bi, ci), 0)),
        out_shape=jax.ShapeDtypeStruct((batch * seq, vd), BF16),
        scratch_shapes=[pltpu.VMEM((heads, GLA_HEAD_V, GLA_HEAD_K), F32)],
        compiler_params=_params("parallel", "arbitrary"),
        name="gla",
    )(proj, proj, proj, proj, lr, w2, b2, ng)


def _diff_attn_kernel(lam_ref, q_ref, k_ref, v_ref, g_ref, ng_ref, o_ref,
                      vt_ref, qt_ref, qtn_ref, s_ref, p_ref, al_ref, m_ref, acc_ref, *, lam_init, tq, tk):
    seq = q_ref.shape[0]
    dv = DIFF_HEAD_V
    nkb = seq // tk
    assert tq == 2 * tk

    vt_ref[0:nkb, dv:, :] = jnp.ones((nkb, BF16_ROWS, tk), BF16)
    vt_ref[nkb] = jnp.zeros(vt_ref.shape[1:], BF16)

    def vt_body(c, carry):
        blk = v_ref[pl.ds(pl.multiple_of(c * tk, tk), tk), :]
        vt_ref[c, 0:dv, :] = blk.astype(F32).T.astype(BF16)
        return carry
    lax.fori_loop(0, nkb, vt_body, 0)
    p_ref[1] = jnp.zeros_like(p_ref[1])
    acc_ref[...] = jnp.zeros_like(acc_ref)

    lam_p = lam_ref[...]
    lam = (jnp.exp(jnp.sum(lam_p[0:1] * lam_p[1:2], axis=-1, keepdims=True))
           - jnp.exp(jnp.sum(lam_p[2:3] * lam_p[3:4], axis=-1, keepdims=True)) + lam_init)
    drow = lax.broadcasted_iota(jnp.int32, (dv, tq), 0)
    kpos = lax.broadcasted_iota(jnp.int32, (tk, tq), 0)
    qpos = lax.broadcasted_iota(jnp.int32, (tk, tq), 1)

    def rows_at(start, size):
        return pl.ds(start if isinstance(start, int) else pl.multiple_of(start, size), size)

    def qk(n, slot, mask):
        kb = k_ref[rows_at(n * tk, tk), :]
        for mp in range(2):
            s = _dot(kb, qt_ref[mp])
            if mask is not None:
                s = jnp.where(mask, s, -jnp.inf)
            s_ref[slot, mp] = s

    def softmax(slot):
        for mp in range(2):
            s = s_ref[slot, mp]
            m_old = m_ref[mp]
            m_new = jnp.maximum(m_old, jnp.max(s, axis=0, keepdims=True))
            al_ref[slot, mp] = jnp.exp2(m_old - m_new)
            p_ref[slot, mp] = jnp.exp2(s - m_new).astype(BF16)
            m_ref[mp] = m_new

    def pv(n, slot):
        if isinstance(n, int):
            vt = vt_ref[nkb if n < 0 else n]
        else:
            vt = vt_ref[jnp.where(n < 0, nkb, n)]
        return [_dot(vt, p_ref[slot, mp]) for mp in range(2)]

    def accumulate(slot, pvs):
        for mp in range(2):
            acc_ref[mp] = al_ref[slot, mp] * acc_ref[mp] + pvs[mp]

    def step(n, parity, next_mask, has_next=True):
        pvs = pv(n - 1, 1 - parity)
        if has_next:
            qk(n + 1, 1 - parity, next_mask)
        softmax(parity)
        accumulate(1 - parity, pvs)

    def finish(i):
        qsl = rows_at(i * tq, tq)
        a0 = acc_ref[0]
        a1 = acc_ref[1]
        o_t = a0[0:dv] / a0[dv:dv + 1] - lam * (a1[0:dv] / a1[dv:dv + 1])
        gate = g_ref[qsl, :].astype(F32)
        y = _rmsnorm(o_t.T, ng_ref[...]) * (1.0 - lam_init) * (gate * _sigmoid(gate))
        o_ref[qsl, :] = y.astype(o_ref.dtype)

    def build_qt(dst_ref, i):
        qf_t = (q_ref[rows_at(i * tq, tq), :].astype(F32) * (DIFF_HEAD_DIM ** -0.5 * LOG2E)).T
        dst_ref[0] = jnp.where(drow < DIFF_HEAD_DIM, qf_t, 0.0).astype(BF16)
        dst_ref[1] = jnp.where(drow >= DIFF_HEAD_DIM, qf_t, 0.0).astype(BF16)

    def q_block(i, first):
        if first:
            build_qt(qt_ref, 0)
        else:
            qt_ref[...] = qtn_ref[...]

        diag0 = kpos <= qpos
        diag1 = (kpos + tk) <= qpos
        qk(0, 0, diag0 if first else None)
        if not first:
            finish(i - 1)
        m_ref[...] = jnp.full_like(m_ref, -jnp.inf)
        al_ref[1] = jnp.zeros(al_ref.shape[1:], F32)

        if not first:
            def quad(qq, c):
                for u in range(4):
                    step(4 * qq + u, u % 2, None)
                return c
            quads = lax.shift_right_logical(i - 1, 1)
            lax.fori_loop(0, quads, quad, 0)

            @pl.when(((i - 1) & 1) == 1)
            def _():
                step(4 * quads, 0, None)
                step(4 * quads + 1, 1, None)

            step(2 * i - 2, 0, None)
            step(2 * i - 1, 1, diag0)

        last = seq // tq - 1
        nxt = min(i + 1, last) if isinstance(i, int) else jnp.minimum(i + 1, last)
        build_qt(qtn_ref, nxt)
        step(2 * i, 0, diag1)
        step(2 * i + 1, 1, None, has_next=False)
        accumulate(1, pv(2 * i + 1, 1))

    q_block(0, True)

    def later_block(i, carry):
        q_block(i, False)
        return carry
    lax.fori_loop(1, seq // tq, later_block, 0)
    finish(seq // tq - 1)


def _diff_attn(proj, lam_p, ng, l, *, batch, seq, d_model, lam_init, tq, tk):
    heads = d_model // DIFF_HEAD_V
    per = d_model // DIFF_HEAD_V
    qb, kb, vb, gb = 3 * per, 4 * per, 5 * per, 6 * per
    head_cols = lambda off: pl.BlockSpec((seq, DIFF_HEAD_V), lambda b, h: (b, off + h))
    dve = DIFF_HEAD_V + BF16_ROWS
    return pl.pallas_call(
        functools.partial(_diff_attn_kernel, lam_init=lam_init, tq=tq, tk=tk),
        grid=(batch, heads),
        in_specs=[
            pl.BlockSpec((None, 4, DIFF_HEAD_DIM), lambda b, h: (l, 0, 0)),
            head_cols(qb), head_cols(kb), head_cols(vb), head_cols(gb),
            _layer_row(l, DIFF_HEAD_V),
        ],
        out_specs=head_cols(0),
        out_shape=jax.ShapeDtypeStruct((batch * seq, d_model), BF16),
        scratch_shapes=[
            pltpu.VMEM((seq // tk + 1, dve, tk), BF16),
            pltpu.VMEM((2, DIFF_HEAD_V, tq), BF16),
            pltpu.VMEM((2, DIFF_HEAD_V, tq), BF16),
            pltpu.VMEM((2, 2, tk, tq), F32),
            pltpu.VMEM((2, 2, tk, tq), BF16),
            pltpu.VMEM((2, 2, 1, tq), F32),
            pltpu.VMEM((2, 1, tq), F32),
            pltpu.VMEM((2, dve, tq), F32),
        ],
        compiler_params=_params("parallel", "parallel"),
        name="diff_attn",
    )(lam_p, proj, proj, proj, proj, ng)


def _merge_kernel(oa_ref, ob_ref, wa_ref, wb_ref, ma_ref, mb_ref, o_ref, w_ref):
    @pl.when(pl.program_id(1) == 0)
    def _():
        rows = wa_ref.shape[0]
        _cast_rows(w_ref.at[0], lambda sl: wa_ref[sl, :], rows)
        _cast_rows(w_ref.at[1], lambda sl: wb_ref[sl, :], rows)

    ya = _dot(oa_ref[...], w_ref[0])
    yb = _dot(ob_ref[...], w_ref[1])
    merged = _sigmoid(ma_ref[...].astype(F32)) * ya + _sigmoid(mb_ref[...].astype(F32)) * yb
    o_ref[...] = merged.astype(o_ref.dtype)


def _merge(o_a, o_b, w_a, w_b, proj, l, *, d_model, tm, tn):
    m = o_a.shape[0]
    nb = d_model // tn
    return pl.pallas_call(
        _merge_kernel,
        grid=(nb, m // tm),
        in_specs=[
            pl.BlockSpec((tm, d_model), lambda j, i: (i, 0)),
            pl.BlockSpec((tm, d_model), lambda j, i: (i, 0)),
            pl.BlockSpec((None, d_model, tn), lambda j, i: (l, 0, j)),
            pl.BlockSpec((None, d_model, tn), lambda j, i: (l, 0, j)),
            pl.BlockSpec((tm, tn), lambda j, i: (i, 7 * nb + j)),
            pl.BlockSpec((tm, tn), lambda j, i: (i, 8 * nb + j)),
        ],
        out_specs=pl.BlockSpec((tm, tn), lambda j, i: (i, j)),
        out_shape=jax.ShapeDtypeStruct((m, d_model), BF16),
        scratch_shapes=[pltpu.VMEM((2, d_model, tn), BF16)],
        compiler_params=_params("parallel", "arbitrary"),
        name="merge",
    )(o_a, o_b, w_a, w_b, proj, proj)


def _outproj_kernel(m_ref, w_ref, g_ref, gn_ref, x_ref, o_ref, *rest, emit_h):
    w_bf = rest[-1]

    @pl.when(pl.program_id(0) == 0)
    def _():
        _cast_rows(w_bf, lambda sl: w_ref[sl, :], w_ref.shape[0])

    x_new = x_ref[...] + _rmsnorm(_dot(m_ref[...], w_bf[...]), g_ref[...])
    o_ref[...] = x_new
    if emit_h:
        rest[0][...] = _rmsnorm(x_new, gn_ref[...]).astype(BF16)


def _outproj(merged, w_out, g, g_pre, x2, l, l_next, *, tm, emit_h):
    m, d = x2.shape
    rows = pl.BlockSpec((tm, d), lambda i: (i, 0))
    out_specs = [rows] + ([rows] if emit_h else [])
    out_shape = [jax.ShapeDtypeStruct((m, d), F32)] + ([jax.ShapeDtypeStruct((m, d), BF16)] if emit_h else [])
    res = pl.pallas_call(
        functools.partial(_outproj_kernel, emit_h=emit_h),
        grid=(m // tm,),
        in_specs=[rows,
                  pl.BlockSpec((None, d, d), lambda i: (l, 0, 0), pipeline_mode=pl.Buffered(1)),
                  _layer_row(l, d), _layer_row(l_next, d), rows],
        out_specs=out_specs,
        out_shape=out_shape,
        scratch_shapes=[pltpu.VMEM((d, d), BF16)],
        compiler_params=_params("arbitrary"),
        name="outproj",
    )(merged, w_out, g, g_pre, x2)
    return (res[0], res[1]) if emit_h else (res[0], None)


def _tile(n, pref):
    t = min(n, pref)
    while n % t:
        t //= 2
    return t


def kernel(x, pre_norm_g, post_norm_g, w_in, gla_gk_w2, gla_gk_b, gla_norm_g, diff_lambda,
           diff_norm_g, w_branch_a, w_branch_b, w_out):
    batch, seq, d = x.shape
    depth = w_in.shape[0]
    m = batch * seq
    w2 = jnp.pad(gla_gk_w2, ((0, 0), (0, LANES - GK_RANK), (0, 0))).astype(BF16)
    w_in_t = jnp.swapaxes(w_in, 1, 2)
    stack = lambda p: p.reshape(depth, 1, p.shape[-1])
    pre_g, post_g = stack(pre_norm_g), stack(post_norm_g)
    gk_b, gla_g, diff_g = stack(gla_gk_b), stack(gla_norm_g), stack(diff_norm_g)

    tm = _tile(m, 1024)
    tq = _tile(seq, ATTN_TQ)
    x2 = x.reshape(m, d)
    h = _prenorm(x2, pre_g, 0, tm=_tile(m, 512))
    for l in range(depth):
        lam_init = 0.8 - 0.6 * math.exp(-0.3 * l)
        last = l + 1 == depth
        proj = _inproj(h, w_in_t, l, tm=tm, tn=_tile(3 * d, 1024))
        lr = _lrproj(h, w_in_t, l, tm=tm)
        o_a = _gla(proj, lr, w2, gk_b, gla_g, l, batch=batch, seq=seq, d_model=d,
                   chunk=_tile(seq, GLA_CHUNK))
        o_b = _diff_attn(proj, diff_lambda, diff_g, l, batch=batch, seq=seq, d_model=d,
                         lam_init=lam_init, tq=tq, tk=tq // 2)
        merged = _merge(o_a, o_b, w_branch_a, w_branch_b, proj, l, d_model=d, tm=tm, tn=_tile(d, 512))
        x2, h = _outproj(merged, w_out, post_g, pre_g, x2, l, (l + 1) % depth,
                         tm=_tile(m, 256), emit_h=not last)
    return x2.reshape(batch, seq, d)
```
